```python
import jax, jax.numpy as jnp
from jax import lax
import numpy as np

D_MODEL = 4096
BATCH = 4
SEQ = 4096
DEPTH = 4

GRID_W = 64
HEAD_DIM = 128
ATTN_WIDTH = D_MODEL // 2
ATTN_HEADS = ATTN_WIDTH // HEAD_DIM
ATTN_KV_HEADS = ATTN_HEADS // 4
ATTN_GROUP = ATTN_HEADS // ATTN_KV_HEADS
KV_WIDTH = ATTN_KV_HEADS * HEAD_DIM
Q_BLOCK = 128
RET_WIDTH = D_MODEL - ATTN_WIDTH
RET_V_DIM = 256
RET_HEADS = RET_WIDTH // RET_V_DIM
RET_QK_DIM = 128
RET_QK_WIDTH = RET_HEADS * RET_QK_DIM
RET_CHUNK = 128
RET_DECAY_BASE_EXP = 5
MIX_WIDTH = ATTN_WIDTH + RET_WIDTH
ROPE_THETA = 10000.0
EPS = 1e-6

IN_SPLITS = (ATTN_WIDTH, KV_WIDTH, KV_WIDTH, ATTN_WIDTH, RET_QK_WIDTH, RET_QK_WIDTH, RET_WIDTH, RET_WIDTH)
IN_WIDTH = sum(IN_SPLITS)
SPLIT_POINTS = tuple(int(p) for p in np.cumsum(IN_SPLITS)[:-1])

kernel_name = 'hymba_gqa_axialrope_biretention_encoder'


def rmsnorm(x, g):
    xf = x.astype(jnp.float32)
    y = xf * lax.rsqrt(jnp.mean(xf * xf, axis=-1, keepdims=True) + EPS)
    return (y * g.astype(jnp.float32)).astype(x.dtype)


def axial_rope_tables(seq_len):
    rows = seq_len // GRID_W
    row = jnp.repeat(jnp.arange(rows), GRID_W).astype(jnp.float32)
    col = jnp.tile(jnp.arange(GRID_W), rows).astype(jnp.float32)
    axis_dim = HEAD_DIM // 2
    inv = ROPE_THETA ** (-jnp.arange(0, axis_dim, 2, dtype=jnp.float32) / axis_dim)
    ang_r = row[:, None] * inv[None, :]
    ang_c = col[:, None] * inv[None, :]
    return (jnp.cos(ang_r), jnp.sin(ang_r), jnp.cos(ang_c), jnp.sin(ang_c))


def _rotate(xp, cos, sin):
    x1, x2 = jnp.split(xp, 2, axis=-1)
    c = cos[None, :, None, :]
    s = sin[None, :, None, :]
    return jnp.concatenate([x1 * c - x2 * s, x1 * s + x2 * c], axis=-1)


def apply_axial_rope(x, rope):
    cos_r, sin_r, cos_c, sin_c = rope
    xf = x.astype(jnp.float32)
    half = x.shape[-1] // 2
    out = jnp.concatenate([_rotate(xf[..., :half], cos_r, sin_r),
                           _rotate(xf[..., half:], cos_c, sin_c)], axis=-1)
    return out.astype(x.dtype)


def gqa_attention(q, k, v):
    B, S = q.shape[0], q.shape[1]
    nb = S // Q_BLOCK
    qb = q.reshape(B, nb, Q_BLOCK, ATTN_KV_HEADS, ATTN_GROUP, HEAD_DIM).transpose(1, 0, 3, 4, 2, 5)
    kt = k.transpose(0, 2, 1, 3)
    vt = v.transpose(0, 2, 1, 3)

    def block(qblk):
        s = jnp.einsum('bkgqd,bksd->bkgqs', qblk, kt).astype(jnp.float32)
        p = jax.nn.softmax(s, axis=-1).astype(vt.dtype)
        return jnp.einsum('bkgqs,bksd->bkgqd', p, vt)

    o = lax.map(block, qb)
    return o.transpose(1, 0, 4, 2, 3, 5).reshape(B, S, ATTN_WIDTH)


def retention_chunkwise(q, k, v, log_g, include_diag):
    q = q.astype(jnp.float32)
    k = k.astype(jnp.float32)
    v = v.astype(jnp.float32)
    log_g = log_g.astype(jnp.float32)
    B, H, S, dk = q.shape
    dv = v.shape[-1]
    C = RET_CHUNK
    n = S // C
    qc = q.reshape(B, H, n, C, dk).transpose(2, 0, 1, 3, 4)
    kc = k.reshape(B, H, n, C, dk).transpose(2, 0, 1, 3, 4)
    vc = v.reshape(B, H, n, C, dv).transpose(2, 0, 1, 3, 4)
    idx = jnp.arange(C, dtype=jnp.float32)
    diff = idx[:, None] - idx[None, :]
    mask = (diff >= 0) if include_diag else (diff > 0)
    safe = jnp.where(mask, diff, 0.0)
    decay_in = jnp.where(mask[None], jnp.exp(log_g[:, None, None] * safe[None]), 0.0)
    xi = jnp.exp(log_g[:, None] * (idx + 1.0)[None])[..., None]
    zeta = jnp.exp(log_g[:, None] * (C - 1.0 - idx)[None])[..., None]
    g_chunk = jnp.exp(log_g * C)[:, None, None]

    def step(state, inp):
        qj, kj, vj = inp
        scores = jnp.einsum('bhqd,bhkd->bhqk', qj, kj) * decay_in
        inner = jnp.einsum('bhqk,bhke->bhqe', scores, vj)
        cross = jnp.einsum('bhqd,bhde->bhqe', qj * xi, state)
        state = g_chunk * state + jnp.einsum('bhkd,bhke->bhde', kj * zeta, vj)
        return state, inner + cross

    state0 = jnp.zeros((B, H, dk, dv), jnp.float32)
    _, out = lax.scan(step, state0, (qc, kc, vc))
    return out.transpose(1, 2, 0, 3, 4).reshape(B, H, S, dv)


def bidirectional_retention(q, k, v, log_gf, log_gb):
    fwd = retention_chunkwise(q, k, v, log_gf, True)
    bwd = retention_chunkwise(jnp.flip(q, 2), jnp.flip(k, 2), jnp.flip(v, 2), log_gb, False)
    return fwd + jnp.flip(bwd, 2)


def setup_inputs(seed: int = 0) -> dict:
    key = jax.random.key(seed)
    ks = jax.random.split(key, 12)
    x = jax.random.normal(ks[0], (BATCH, SEQ, D_MODEL), jnp.float32)
    norm_w = 1.0 + 0.02 * jax.random.normal(ks[1], (DEPTH, D_MODEL), jnp.float32)
    w_in = jax.random.normal(ks[2], (DEPTH, D_MODEL, IN_WIDTH), jnp.float32) * (D_MODEL ** -0.5)
    q_norm = 1.0 + 0.02 * jax.random.normal(ks[3], (DEPTH, HEAD_DIM), jnp.float32)
    k_norm = 1.0 + 0.02 * jax.random.normal(ks[4], (DEPTH, HEAD_DIM), jnp.float32)
    base = jnp.asarray(np.log(2.0 ** (RET_DECAY_BASE_EXP + np.arange(RET_HEADS)) - 1.0), jnp.float32)
    ret_decay_fwd = base[None] + 0.1 * jax.random.normal(ks[5], (DEPTH, RET_HEADS), jnp.float32)
    ret_decay_bwd = base[None] + 0.1 * jax.random.normal(ks[6], (DEPTH, RET_HEADS), jnp.float32)
    ret_norm = 1.0 + 0.02 * jax.random.normal(ks[7], (DEPTH, RET_HEADS, RET_V_DIM), jnp.float32)
    w_out = jax.random.normal(ks[8], (DEPTH, MIX_WIDTH, D_MODEL), jnp.float32) * (MIX_WIDTH ** -0.5)
    final_norm = 1.0 + 0.02 * jax.random.normal(ks[9], (D_MODEL,), jnp.float32)
    return {'x': x, 'norm_w': norm_w, 'w_in': w_in, 'q_norm': q_norm, 'k_norm': k_norm,
            'ret_decay_fwd': ret_decay_fwd, 'ret_decay_bwd': ret_decay_bwd, 'ret_norm': ret_norm,
            'w_out': w_out, 'final_norm': final_norm}


def reference(x, norm_w, w_in, q_norm, k_norm, ret_decay_fwd, ret_decay_bwd, ret_norm, w_out, final_norm):
    B, S, _ = x.shape
    rope = axial_rope_tables(S)
    attn_scale = HEAD_DIM ** -0.5
    ret_scale = RET_QK_DIM ** -0.5
    for l in range(DEPTH):
        h = rmsnorm(x, norm_w[l])
        proj = h @ w_in[l]
        aq, ak, av, ag, rq, rk, rv, rg = jnp.split(proj, SPLIT_POINTS, axis=-1)

        aq = apply_axial_rope(rmsnorm(aq.reshape(B, S, ATTN_HEADS, HEAD_DIM), q_norm[l]), rope) * attn_scale
        ak = apply_axial_rope(rmsnorm(ak.reshape(B, S, ATTN_KV_HEADS, HEAD_DIM), k_norm[l]), rope)
        av = av.reshape(B, S, ATTN_KV_HEADS, HEAD_DIM)
        a_out = (jax.nn.silu(ag) * gqa_attention(aq, ak, av)).astype(x.dtype)

        rq = apply_axial_rope(rq.reshape(B, S, RET_HEADS, RET_QK_DIM), rope).transpose(0, 2, 1, 3)
        rk = (apply_axial_rope(rk.reshape(B, S, RET_HEADS, RET_QK_DIM), rope) * ret_scale).transpose(0, 2, 1, 3)
        rv = rv.reshape(B, S, RET_HEADS, RET_V_DIM).transpose(0, 2, 1, 3)
        log_gf = jax.nn.log_sigmoid(ret_decay_fwd[l].astype(jnp.float32))
        log_gb = jax.nn.log_sigmoid(ret_decay_bwd[l].astype(jnp.float32))
        r = bidirectional_retention(rq, rk, rv, log_gf, log_gb).transpose(0, 2, 1, 3)
        r = rmsnorm(r, ret_norm[l]).reshape(B, S, RET_WIDTH)
        r_out = (jax.nn.silu(rg.astype(jnp.float32)) * r).astype(x.dtype)

        x = x + jnp.concatenate([a_out, r_out], axis=-1) @ w_out[l]
    return rmsnorm(x, final_norm)
```

```python
import functools
import math

import jax
import jax.numpy as jnp
from jax import lax
from jax.experimental import pallas as pl
from jax.experimental.pallas import tpu as pltpu

F32 = jnp.float32
BF16 = jnp.bfloat16

GRID_W = 64
HEAD_DIM = 128
ATTN_GROUP = 4
RET_V_DIM = 256
RET_QK_DIM = 128
ROPE_THETA = 10000.0
EPS = 1e-6
LOG2E = math.log2(math.e)

LANES = 128
VMEM_LIMIT = 56 * 1024 * 1024

NT_DIMS = (((1,), (1,)), ((), ()))


def _cparams(*sem):
    return pltpu.CompilerParams(dimension_semantics=sem, vmem_limit_bytes=VMEM_LIMIT)


def _prologue_kernel(x_ref, g_ref, xg_ref, ssq_ref):
    x = x_ref[...]
    xg_ref[...] = (x * g_ref[...]).astype(BF16)
    ssq_ref[...] = jnp.broadcast_to(jnp.sum(x * x, axis=-1, keepdims=True), ssq_ref.shape)


def _prologue(x2, g):
    m, d = x2.shape
    tm = min(256, m)
    return pl.pallas_call(
        _prologue_kernel,
        grid=(m // tm,),
        in_specs=[pl.BlockSpec((tm, d), lambda i: (i, 0)),
                  pl.BlockSpec((1, d), lambda i: (0, 0))],
        out_specs=[pl.BlockSpec((tm, d), lambda i: (i, 0)),
                   pl.BlockSpec((tm, LANES), lambda i: (i, 0))],
        out_shape=[jax.ShapeDtypeStruct((m, d), BF16),
                   jax.ShapeDtypeStruct((m, LANES), F32)],
        compiler_params=_cparams("parallel"),
        name="prologue",
    )(x2, g)


def _rope(y, cos, sina, sinb):
    return y * cos + pltpu.roll(y, 96, 1) * sina + pltpu.roll(y, 32, 1) * sinb


def _silu(y):
    return y * (1.0 / (1.0 + jnp.exp(-y)))


def _inproj_kernel(sections, d_model, rc, xg_ref, w_ref, ssq_ref, cos_ref, sina_ref, sinb_ref,
                   gq_ref, gk_ref, o_ref, acc_ref):
    j = pl.program_id(1)
    tm, tn = acc_ref.shape
    acc_ref[...] = jnp.dot(xg_ref[...], w_ref[...], preferred_element_type=F32)

    def epilogue(kind):
        def chunk(c, carry):
            rows = pl.ds(pl.multiple_of(c * rc, rc), rc)
            r = lax.rsqrt(ssq_ref[rows, :] * (1.0 / d_model) + EPS)
            if kind in ("aq", "ak", "rq", "rk"):
                cos, sina, sinb = cos_ref[rows, :], sina_ref[rows, :], sinb_ref[rows, :]
            for g in range(tn // LANES):
                cols = slice(g * LANES, (g + 1) * LANES)
                y = acc_ref[rows, cols] * r
                if kind in ("aq", "ak"):
                    gain = gq_ref[...] if kind == "aq" else gk_ref[...]
                    ms = jnp.mean(y * y, axis=-1, keepdims=True)
                    y = _rope(y * lax.rsqrt(ms + EPS) * gain, cos, sina, sinb)
                elif kind == "rq":
                    y = _rope(y, cos, sina, sinb)
                elif kind == "rk":
                    y = _rope(y * (RET_QK_DIM ** -0.5), cos, sina, sinb)
                elif kind == "gate":
                    y = _silu(y)
                o_ref[rows, cols] = y.astype(BF16)
            return carry
        lax.fori_loop(0, tm // rc, chunk, 0)

    for kind, lo, hi in sections:
        pl.when(jnp.logical_and(j >= lo, j < hi))(functools.partial(epilogue, kind))


def _inproj(xg, w, ssq, cos, sina, sinb, gq, gk, *, seq, sections, tn):
    m, d = xg.shape
    n = w.shape[1]
    tm = min(1024, seq)
    rc = min(128, tm)
    sb = seq // tm
    kern = functools.partial(_inproj_kernel, sections, d, rc)
    tab = pl.BlockSpec((tm, LANES), lambda i, j: (i % sb, 0))
    return pl.pallas_call(
        kern,
        grid=(m // tm, n // tn),
        in_specs=[pl.BlockSpec((tm, d), lambda i, j: (i, 0)),
                  pl.BlockSpec((d, tn), lambda i, j: (0, j)),
                  pl.BlockSpec((tm, LANES), lambda i, j: (i, 0)),
                  tab, tab, tab,
                  pl.BlockSpec((1, LANES), lambda i, j: (0, 0)),
                  pl.BlockSpec((1, LANES), lambda i, j: (0, 0))],
        out_specs=pl.BlockSpec((tm, tn), lambda i, j: (i, j)),
        out_shape=jax.ShapeDtypeStruct((m, n), BF16),
        scratch_shapes=[pltpu.VMEM((tm, tn), F32)],
        compiler_params=_cparams("parallel", "arbitrary"),
        name="inproj",
    )(xg, w, ssq, cos, sina, sinb, gq, gk)


def _transpose_via_mxu(a):
    rows = lax.broadcasted_iota(jnp.int32, (LANES, LANES), 0)
    cols = lax.broadcasted_iota(jnp.int32, (LANES, LANES), 1)
    eye = jnp.where(rows == cols, 1.0, 0.0).astype(BF16)
    return lax.dot_general(eye, a, NT_DIMS, preferred_element_type=F32).astype(BF16)


def _attn_kernel(tq, tk, q_ref, k_ref, v_ref, g_ref, o_ref, vt_ref):
    seq = q_ref.shape[0]
    nq = ATTN_GROUP * tq
    vt_ref[...] = _transpose_via_mxu(v_ref[...])

    def qblock(qi, carry):
        rows = pl.ds(pl.multiple_of(qi * tq, tq), tq)
        qb = q_ref[rows, :]
        qs = jnp.concatenate([qb[:, g * LANES:(g + 1) * LANES] for g in range(ATTN_GROUP)], axis=0)

        def kvstep(c, st):
            m, l, acc = st
            c0 = pl.multiple_of(c * tk, tk)
            s = lax.dot_general(k_ref[pl.ds(c0, tk), :], qs, NT_DIMS,
                                preferred_element_type=F32)
            m_new = jnp.maximum(m, jnp.max(s, axis=0, keepdims=True))
            alpha = jnp.exp2(m - m_new)
            p = jnp.exp2(s - m_new)
            l = alpha * l + jnp.sum(p, axis=0, keepdims=True)
            pv = jnp.dot(vt_ref[:, pl.ds(c0, tk)], p.astype(BF16), preferred_element_type=F32)
            return m_new, l, alpha * acc + pv

        init = (jnp.full((1, nq), -1e30, F32), jnp.zeros((1, nq), F32), jnp.zeros((HEAD_DIM, nq), F32))
        _, l, acc = lax.fori_loop(0, seq // tk, kvstep, init)
        o = acc * (1.0 / l)
        for g in range(ATTN_GROUP):
            cols = slice(g * LANES, (g + 1) * LANES)
            og = o[:, g * tq:(g + 1) * tq].T
            o_ref[rows, cols] = (g_ref[rows, cols].astype(F32) * og).astype(BF16)
        return carry

    lax.fori_loop(0, seq // tq, qblock, 0)


def _attention(proj, *, batch, seq, attn_width, kv_width):
    kvh = kv_width // HEAD_DIM
    gw = ATTN_GROUP * HEAD_DIM
    tq = min(256, seq)
    tk = min(512, seq)
    k_off = attn_width // HEAD_DIM
    v_off = (attn_width + kv_width) // HEAD_DIM
    g_off = (attn_width + 2 * kv_width) // gw
    assert (attn_width + 2 * kv_width) % gw == 0
    return pl.pallas_call(
        functools.partial(_attn_kernel, tq, tk),
        grid=(batch, kvh),
        in_specs=[pl.BlockSpec((seq, gw), lambda b, h: (b, h)),
                  pl.BlockSpec((seq, HEAD_DIM), lambda b, h: (b, k_off + h)),
                  pl.BlockSpec((seq, HEAD_DIM), lambda b, h: (b, v_off + h)),
                  pl.BlockSpec((seq, gw), lambda b, h: (b, g_off + h))],
        out_specs=pl.BlockSpec((seq, gw), lambda b, h: (b, h)),
        out_shape=jax.ShapeDtypeStruct((batch * seq, attn_width), BF16),
        scratch_shapes=[pltpu.VMEM((HEAD_DIM, seq), BF16)],
        compiler_params=_cparams("parallel", "parallel"),
        name="attention",
    )(proj, proj, proj, proj)


def _log_sigmoid(x):
    return jnp.minimum(x, 0.0) - jnp.log1p(jnp.exp(-jnp.abs(x)))


def _retention_kernel(ck, q_ref, k_ref, v_ref, g_ref, df_ref, db_ref, gn_ref, o_ref, kt_ref, sb_ref):
    seq = q_ref.shape[0]
    nchunk = seq // ck
    lgf = _log_sigmoid(df_ref[...])[:, :1]
    lgb = _log_sigmoid(db_ref[...])[:, :1]
    kt_ref[...] = _transpose_via_mxu(k_ref[...])

    diff = (lax.broadcasted_iota(jnp.int32, (ck, ck), 0)
            - lax.broadcasted_iota(jnp.int32, (ck, ck), 1)).astype(F32)
    decay = jnp.exp(jnp.where(diff >= 0, lgf * diff, -lgb * diff))
    rowi = lax.broadcasted_iota(jnp.int32, (ck, RET_QK_DIM), 0).astype(F32)
    xi_f = jnp.exp(lgf * (rowi + 1.0))
    xi_b = jnp.exp(lgb * (ck - rowi))
    coli = lax.broadcasted_iota(jnp.int32, (RET_QK_DIM, ck), 1).astype(F32)
    zt_f = jnp.exp(lgf * (ck - 1.0 - coli))
    zt_b = jnp.exp(lgb * coli)
    gc_f = jnp.exp(lgf * ck)
    gc_b = jnp.exp(lgb * ck)

    def kz(c0, zt):
        return (kt_ref[:, pl.ds(c0, ck)].astype(F32) * zt).astype(BF16)

    def bstep(t, state):
        jc = nchunk - 1 - t
        c0 = pl.multiple_of(jc * ck, ck)
        sb_ref[jc] = state.astype(BF16)
        upd = jnp.dot(kz(c0, zt_b), v_ref[pl.ds(c0, ck), :], preferred_element_type=F32)
        return gc_b * state + upd

    lax.fori_loop(0, nchunk, bstep, jnp.zeros((RET_QK_DIM, RET_V_DIM), F32))

    def fstep(jc, state):
        c0 = pl.multiple_of(jc * ck, ck)
        rows = pl.ds(c0, ck)
        q = q_ref[rows, :]
        v = v_ref[rows, :]
        s = jnp.dot(q, kt_ref[:, pl.ds(c0, ck)], preferred_element_type=F32)
        out = jnp.dot((s * decay).astype(BF16), v, preferred_element_type=F32)
        qf = q.astype(F32)
        out = out + jnp.dot((qf * xi_f).astype(BF16), state.astype(BF16), preferred_element_type=F32)
        out = out + jnp.dot((qf * xi_b).astype(BF16), sb_ref[jc], preferred_element_type=F32)
        ms = jnp.mean(out * out, axis=-1, keepdims=True)
        y = out * lax.rsqrt(ms + EPS) * gn_ref[...]
        o_ref[rows, :] = (g_ref[rows, :].astype(F32) * y).astype(BF16)
        upd = jnp.dot(kz(c0, zt_f), v, preferred_element_type=F32)
        return gc_f * state + upd

    lax.fori_loop(0, nchunk, fstep, jnp.zeros((RET_QK_DIM, RET_V_DIM), F32))


def _retention(proj, dfw, dbw, gnorm, *, batch, seq, offs, ret_heads):
    off_q, off_k, off_v, off_g = offs
    ck = min(256, seq)
    vec = lambda w: pl.BlockSpec((None, 1, w), lambda b, h: (h, 0, 0))
    return pl.pallas_call(
        functools.partial(_retention_kernel, ck),
        grid=(batch, ret_heads),
        in_specs=[pl.BlockSpec((seq, RET_QK_DIM), lambda b, h: (b, off_q // RET_QK_DIM + h)),
                  pl.BlockSpec((seq, RET_QK_DIM), lambda b, h: (b, off_k // RET_QK_DIM + h)),
                  pl.BlockSpec((seq, RET_V_DIM), lambda b, h: (b, off_v // RET_V_DIM + h)),
                  pl.BlockSpec((seq, RET_V_DIM), lambda b, h: (b, off_g // RET_V_DIM + h)),
                  vec(LANES), vec(LANES), vec(RET_V_DIM)],
        out_specs=pl.BlockSpec((seq, RET_V_DIM), lambda b, h: (b, h)),
        out_shape=jax.ShapeDtypeStruct((batch * seq, ret_heads * RET_V_DIM), BF16),
        scratch_shapes=[pltpu.VMEM((RET_QK_DIM, seq), BF16),
                        pltpu.VMEM((seq // ck, RET_QK_DIM, RET_V_DIM), BF16)],
        compiler_params=_cparams("parallel", "parallel"),
        name="retention",
    )(proj, proj, proj, proj, dfw, dbw, gnorm)


def _outproj_kernel(emit_next, a_ref, r_ref, wa_ref, wr_ref, x_ref, *rest):
    if emit_next:
        g_ref, xo_ref, xg_ref, ssq_ref = rest
    else:
        (xo_ref,) = rest
    j = pl.program_id(1)
    acc = jnp.dot(a_ref[...], wa_ref[...], preferred_element_type=F32)
    acc = acc + jnp.dot(r_ref[...], wr_ref[...], preferred_element_type=F32)
    xn = x_ref[...] + acc
    xo_ref[...] = xn
    if emit_next:
        xg_ref[...] = (xn * g_ref[...]).astype(BF16)
        part = jnp.broadcast_to(jnp.sum(xn * xn, axis=-1, keepdims=True), ssq_ref.shape)

        @pl.when(j == 0)
        def _():
            ssq_ref[...] = part

        @pl.when(j > 0)
        def _():
            ssq_ref[...] = ssq_ref[...] + part


def _outproj(a, r, w, x2, gnext):
    m, d = x2.shape
    ka, kr = a.shape[1], r.shape[1]
    assert ka == kr and w.shape[0] == ka + kr
    tm = min(1024, m)
    tn = min(512, d)
    emit_next = gnext is not None
    in_specs = [pl.BlockSpec((tm, ka), lambda i, j: (i, 0)),
                pl.BlockSpec((tm, kr), lambda i, j: (i, 0)),
                pl.BlockSpec((ka, tn), lambda i, j: (0, j)),
                pl.BlockSpec((kr, tn), lambda i, j: (1, j)),
                pl.BlockSpec((tm, tn), lambda i, j: (i, j))]
    out_specs = [pl.BlockSpec((tm, tn), lambda i, j: (i, j))]
    out_shape = [jax.ShapeDtypeStruct((m, d), F32)]
    args = [a, r, w, w, x2]
    if emit_next:
        in_specs.append(pl.BlockSpec((1, tn), lambda i, j: (0, j)))
        out_specs += [pl.BlockSpec((tm, tn), lambda i, j: (i, j)),
                      pl.BlockSpec((tm, LANES), lambda i, j: (i, 0))]
        out_shape += [jax.ShapeDtypeStruct((m, d), BF16),
                      jax.ShapeDtypeStruct((m, LANES), F32)]
        args.append(gnext)
    return pl.pallas_call(
        functools.partial(_outproj_kernel, emit_next),
        grid=(m // tm, d // tn),
        in_specs=in_specs,
        out_specs=out_specs,
        out_shape=out_shape,
        input_output_aliases={4: 0},
        compiler_params=_cparams("parallel", "arbitrary"),
        name="outproj",
    )(*args)


def _final_kernel(x_ref, g_ref, o_ref):
    x = x_ref[...]
    ms = jnp.mean(x * x, axis=-1, keepdims=True)
    o_ref[...] = x * lax.rsqrt(ms + EPS) * g_ref[...]


def _final_norm(x2, g):
    m, d = x2.shape
    tm = min(256, m)
    return pl.pallas_call(
        _final_kernel,
        grid=(m // tm,),
        in_specs=[pl.BlockSpec((tm, d), lambda i: (i, 0)),
                  pl.BlockSpec((1, d), lambda i: (0, 0))],
        out_specs=pl.BlockSpec((tm, d), lambda i: (i, 0)),
        out_shape=jax.ShapeDtypeStruct((m, d), F32),
        compiler_params=_cparams("parallel"),
        name="final_norm",
    )(x2, g)


def _rope_tables(seq):
    rows = seq // GRID_W
    row = jnp.repeat(jnp.arange(rows), GRID_W).astype(F32)
    col = jnp.tile(jnp.arange(GRID_W), rows).astype(F32)
    axis_dim = HEAD_DIM // 2
    inv = ROPE_THETA ** (-jnp.arange(0, axis_dim, 2, dtype=F32) / axis_dim)
    ang_r = row[:, None] * inv[None, :]
    ang_c = col[:, None] * inv[None, :]
    cr, sr, cc, sc = jnp.cos(ang_r), jnp.sin(ang_r), jnp.cos(ang_c), jnp.sin(ang_c)
    z = jnp.zeros_like(sr)
    cos = jnp.concatenate([cr, cr, cc, cc], axis=-1)
    sina = jnp.concatenate([-sr, z, -sc, z], axis=-1)
    sinb = jnp.concatenate([z, sr, z, sc], axis=-1)
    return cos, sina, sinb


def kernel(x, norm_w, w_in, q_norm, k_norm, ret_decay_fwd, ret_decay_bwd, ret_norm, w_out, final_norm):
    batch, seq, d = x.shape
    depth = w_in.shape[0]
    attn_width = d // 2
    kv_width = attn_width // ATTN_GROUP
    ret_width = d - attn_width
    ret_heads = ret_width // RET_V_DIM
    ret_qk_width = ret_heads * RET_QK_DIM
    m = batch * seq

    widths = (("aq", attn_width), ("ak", kv_width), ("av", kv_width), ("gate", attn_width),
              ("rq", ret_qk_width), ("rk", ret_qk_width), ("rv", ret_width), ("gate", ret_width))
    tn = min(512, kv_width)
    sections, offs, pos = [], [], 0
    for kind, wdt in widths:
        assert wdt % tn == 0
        sections.append((kind if kind != "av" and kind != "rv" else "plain", pos // tn, (pos + wdt) // tn))
        offs.append(pos)
        pos += wdt
    assert pos == w_in.shape[2]
    ret_offs = tuple(offs[4:8])

    cos, sina, sinb = _rope_tables(seq)
    attn_scale = HEAD_DIM ** -0.5
    w_in_b = w_in.astype(BF16)
    w_out_b = w_out.astype(BF16)
    x2 = x.reshape(m, d)

    xg, ssq = _prologue(x2, norm_w[0].reshape(1, d))
    for l in range(depth):
        gq = (q_norm[l] * (attn_scale * LOG2E)).reshape(1, HEAD_DIM)
        gk = k_norm[l].reshape(1, HEAD_DIM)
        proj = _inproj(xg, w_in_b[l], ssq, cos, sina, sinb, gq, gk, seq=seq, sections=tuple(sections), tn=tn)
        a = _attention(proj, batch=batch, seq=seq, attn_width=attn_width, kv_width=kv_width)
        dfw = jnp.broadcast_to(ret_decay_fwd[l][:, None, None], (ret_heads, 1, LANES))
        dbw = jnp.broadcast_to(ret_decay_bwd[l][:, None, None], (ret_heads, 1, LANES))
        r = _retention(proj, dfw, dbw, ret_norm[l].reshape(ret_heads, 1, RET_V_DIM),
                       batch=batch, seq=seq, offs=ret_offs, ret_heads=ret_heads)
        if l + 1 < depth:
            x2, xg, ssq = _outproj(a, r, w_out_b[l], x2, norm_w[l + 1].reshape(1, d))
        else:
            (x2,) = _outproj(a, r, w_out_b[l], x2, None)
    out = _final_norm(x2, final_norm.reshape(1, d))
    return out.reshape(batch, seq, d)
```

```python
import functools
import math

import jax
import jax.numpy as jnp
from jax import lax
from jax.experimental import pallas as pl
from jax.experimental.pallas import tpu as pltpu

F32 = jnp.float32
BF16 = jnp.bfloat16

GRID_W = 64
HEAD_DIM = 128
ATTN_GROUP = 4
RET_V_DIM = 256
RET_QK_DIM = 128
ROPE_THETA = 10000.0
EPS = 1e-6
LOG2E = math.log2(math.e)

LANES = 128
VMEM_LIMIT = 56 * 1024 * 1024

NT_DIMS = (((1,), (1,)), ((), ()))


def _cparams(*sem):
    return pltpu.CompilerParams(dimension_semantics=sem, vmem_limit_bytes=VMEM_LIMIT)


def _prologue_kernel(x_ref, g_ref, xg_ref, ssq_ref):
    x = x_ref[...]
    xg_ref[...] = (x * g_ref[...]).astype(BF16)
    ssq_ref[...] = jnp.broadcast_to(jnp.sum(x * x, axis=-1, keepdims=True), ssq_ref.shape)


def _prologue(x2, g):
    m, d = x2.shape
    tm = min(256, m)
    return pl.pallas_call(
        _prologue_kernel,
        grid=(m // tm,),
        in_specs=[pl.BlockSpec((tm, d), lambda i: (i, 0)),
                  pl.BlockSpec((1, d), lambda i: (0, 0))],
        out_specs=[pl.BlockSpec((tm, d), lambda i: (i, 0)),
                   pl.BlockSpec((tm, LANES), lambda i: (i, 0))],
        out_shape=[jax.ShapeDtypeStruct((m, d), BF16),
                   jax.ShapeDtypeStruct((m, LANES), F32)],
        compiler_params=_cparams("parallel"),
        name="prologue",
    )(x2, g)


def _rope(y, cos, sina, sinb):
    return y * cos + pltpu.roll(y, 96, 1) * sina + pltpu.roll(y, 32, 1) * sinb


def _silu(y):
    return y * (1.0 / (1.0 + jnp.exp(-y)))


def _inproj_kernel(sections, d_model, rc, xg_ref, w_ref, ssq_ref, cos_ref, sina_ref, sinb_ref,
                   gq_ref, gk_ref, o_ref, acc_ref):
    j = pl.program_id(1)
    tm, tn = acc_ref.shape
    acc_ref[...] = jnp.dot(xg_ref[...], w_ref[...], preferred_element_type=F32)

    def epilogue(kind):
        def chunk(c, carry):
            rows = pl.ds(pl.multiple_of(c * rc, rc), rc)
            r = lax.rsqrt(ssq_ref[rows, :] * (1.0 / d_model) + EPS)
            if kind in ("aq", "ak", "rq", "rk"):
                cos, sina, sinb = cos_ref[rows, :], sina_ref[rows, :], sinb_ref[rows, :]
            for g in range(tn // LANES):
                cols = slice(g * LANES, (g + 1) * LANES)
                y = acc_ref[rows, cols] * r
                if kind in ("aq", "ak"):
                    gain = gq_ref[...] if kind == "aq" else gk_ref[...]
                    ms = jnp.mean(y * y, axis=-1, keepdims=True)
                    y = _rope(y * lax.rsqrt(ms + EPS) * gain, cos, sina, sinb)
                elif kind == "rq":
                    y = _rope(y, cos, sina, sinb)
                elif kind == "rk":
                    y = _rope(y * (RET_QK_DIM ** -0.5), cos, sina, sinb)
                elif kind == "gate":
                    y = _silu(y)
                o_ref[rows, cols] = y.astype(BF16)
            return carry
        lax.fori_loop(0, tm // rc, chunk, 0)

    for kind, lo, hi in sections:
        pl.when(jnp.logical_and(j >= lo, j < hi))(functools.partial(epilogue, kind))


def _inproj(xg, w, layer, ssq, cos, sina, sinb, gq, gk, *, seq, sections, tn):
    m, d = xg.shape
    n = w.shape[2]
    tm = min(1024, seq)
    rc = min(128, tm)
    sb = seq // tm
    kern = functools.partial(_inproj_kernel, sections, d, rc)
    tab = pl.BlockSpec((tm, LANES), lambda i, j: (i % sb, 0))
    return pl.pallas_call(
        kern,
        grid=(m // tm, n // tn),
        in_specs=[pl.BlockSpec((tm, d), lambda i, j: (i, 0)),
                  pl.BlockSpec((None, d, tn), lambda i, j: (layer, 0, j)),
                  pl.BlockSpec((tm, LANES), lambda i, j: (i, 0)),
                  tab, tab, tab,
                  pl.BlockSpec((1, LANES), lambda i, j: (0, 0)),
                  pl.BlockSpec((1, LANES), lambda i, j: (0, 0))],
        out_specs=pl.BlockSpec((tm, tn), lambda i, j: (i, j)),
        out_shape=jax.ShapeDtypeStruct((m, n), BF16),
        scratch_shapes=[pltpu.VMEM((tm, tn), F32)],
        compiler_params=_cparams("parallel", "arbitrary"),
        name="inproj",
    )(xg, w, ssq, cos, sina, sinb, gq, gk)


def _transpose_via_mxu(a):
    rows = lax.broadcasted_iota(jnp.int32, (LANES, LANES), 0)
    cols = lax.broadcasted_iota(jnp.int32, (LANES, LANES), 1)
    eye = jnp.where(rows == cols, 1.0, 0.0).astype(BF16)
    return lax.dot_general(eye, a, NT_DIMS, preferred_element_type=F32).astype(BF16)


SAFE_SHIFT = 50.0
ONES_ROWS = 16


def _attn_kernel(tq, tk, q_ref, k_ref, v_ref, g_ref, o_ref, kaug_ref, vtaug_ref):
    seq = q_ref.shape[0]
    nq = ATTN_GROUP * tq
    k = k_ref[...]
    kaug_ref[:, :LANES] = k
    lane_k = lax.broadcasted_iota(jnp.int32, (seq, LANES), 1)
    kaug_ref[:, LANES:] = jnp.where(lane_k == 0, 1.0, 0.0).astype(BF16)
    vtaug_ref[:HEAD_DIM, :] = _transpose_via_mxu(v_ref[...])
    sub_v = lax.broadcasted_iota(jnp.int32, (ONES_ROWS, seq), 0)
    vtaug_ref[HEAD_DIM:, :] = jnp.where(sub_v == 0, 1.0, 0.0).astype(BF16)
    kf = k.astype(F32)
    kmax2 = jnp.max(jnp.sum(kf * kf, axis=1, keepdims=True), axis=0, keepdims=True)

    def qblock(qi, carry):
        rows = pl.ds(pl.multiple_of(qi * tq, tq), tq)
        qb = q_ref[rows, :]
        qs = jnp.concatenate([qb[:, g * LANES:(g + 1) * LANES] for g in range(ATTN_GROUP)], axis=0)
        qf = qs.astype(F32)
        bound = jnp.sqrt(jnp.sum(qf * qf, axis=1, keepdims=True) * kmax2)
        lane_q = lax.broadcasted_iota(jnp.int32, (nq, LANES), 1)
        qaug = jnp.concatenate([qs, jnp.where(lane_q == 0, -bound, 0.0).astype(BF16)], axis=1)

        def shifted():
            def kvstep(c, acc):
                c0 = pl.multiple_of(c * tk, tk)
                s = lax.dot_general(kaug_ref[pl.ds(c0, tk), :], qaug, NT_DIMS,
                                    preferred_element_type=F32)
                p = jnp.exp2(s).astype(BF16)
                return acc + jnp.dot(vtaug_ref[:, pl.ds(c0, tk)], p, preferred_element_type=F32)

            acc = lax.fori_loop(0, seq // tk, kvstep, jnp.zeros((HEAD_DIM + ONES_ROWS, nq), F32), unroll=True)
            return acc[:HEAD_DIM] * (1.0 / acc[HEAD_DIM:HEAD_DIM + 1])

        def running_max():
            def kvstep(c, st):
                m, l, acc = st
                c0 = pl.multiple_of(c * tk, tk)
                s = lax.dot_general(k_ref[pl.ds(c0, tk), :], qs, NT_DIMS,
                                    preferred_element_type=F32)
                m_new = jnp.maximum(m, jnp.max(s, axis=0, keepdims=True))
                alpha = jnp.exp2(m - m_new)
                p = jnp.exp2(s - m_new)
                l = alpha * l + jnp.sum(p, axis=0, keepdims=True)
                pv = jnp.dot(vtaug_ref[:HEAD_DIM, pl.ds(c0, tk)], p.astype(BF16), preferred_element_type=F32)
                return m_new, l, alpha * acc + pv

            init = (jnp.full((1, nq), -1e30, F32), jnp.zeros((1, nq), F32), jnp.zeros((HEAD_DIM, nq), F32))
            _, l, acc = lax.fori_loop(0, seq // tk, kvstep, init)
            return acc * (1.0 / l)

        o = lax.cond(jnp.max(bound) <= SAFE_SHIFT, shifted, running_max)
        for g in range(ATTN_GROUP):
            cols = slice(g * LANES, (g + 1) * LANES)
            og = o[:, g * tq:(g + 1) * tq].T
            o_ref[rows, cols] = (g_ref[rows, cols].astype(F32) * og).astype(BF16)
        return carry

    lax.fori_loop(0, seq // tq, qblock, 0)


def _attention(proj, *, batch, seq, attn_width, kv_width):
    kvh = kv_width // HEAD_DIM
    gw = ATTN_GROUP * HEAD_DIM
    tq = min(256, seq)
    tk = min(512, seq)
    k_off = attn_width // HEAD_DIM
    v_off = (attn_width + kv_width) // HEAD_DIM
    g_off = (attn_width + 2 * kv_width) // gw
    assert (attn_width + 2 * kv_width) % gw == 0
    return pl.pallas_call(
        functools.partial(_attn_kernel, tq, tk),
        grid=(batch, kvh),
        in_specs=[pl.BlockSpec((seq, gw), lambda b, h: (b, h)),
                  pl.BlockSpec((seq, HEAD_DIM), lambda b, h: (b, k_off + h)),
                  pl.BlockSpec((seq, HEAD_DIM), lambda b, h: (b, v_off + h)),
                  pl.BlockSpec((seq, gw), lambda b, h: (b, g_off + h))],
        out_specs=pl.BlockSpec((seq, gw), lambda b, h: (b, h)),
        out_shape=jax.ShapeDtypeStruct((batch * seq, attn_width), BF16),
        scratch_shapes=[pltpu.VMEM((seq, 2 * HEAD_DIM), BF16),
                        pltpu.VMEM((HEAD_DIM + ONES_ROWS, seq), BF16)],
        compiler_params=_cparams("parallel", "parallel"),
        name="attention",
    )(proj, proj, proj, proj)


def _log_sigmoid(x):
    return jnp.minimum(x, 0.0) - jnp.log1p(jnp.exp(-jnp.abs(x)))


def _retention_kernel(ck, q_ref, k_ref, v_ref, g_ref, df_ref, db_ref, gn_ref, o_ref, kt_ref, sb_ref):
    seq = q_ref.shape[0]
    nchunk = seq // ck
    lgf = _log_sigmoid(df_ref[...])[:, :1]
    lgb = _log_sigmoid(db_ref[...])[:, :1]
    kt_ref[...] = _transpose_via_mxu(k_ref[...])

    diff = (lax.broadcasted_iota(jnp.int32, (ck, ck), 0)
            - lax.broadcasted_iota(jnp.int32, (ck, ck), 1)).astype(F32)
    decay = jnp.exp(jnp.where(diff >= 0, lgf * diff, -lgb * diff))
    rowi = lax.broadcasted_iota(jnp.int32, (ck, RET_QK_DIM), 0).astype(F32)
    xi_f = jnp.exp(lgf * (rowi + 1.0))
    xi_b = jnp.exp(lgb * (ck - rowi))
    coli = lax.broadcasted_iota(jnp.int32, (RET_QK_DIM, ck), 1).astype(F32)
    zt_f = jnp.exp(lgf * (ck - 1.0 - coli))
    zt_b = jnp.exp(lgb * coli)
    gc_f = jnp.exp(lgf * ck)
    gc_b = jnp.exp(lgb * ck)

    def kz(c0, zt):
        return (kt_ref[:, pl.ds(c0, ck)].astype(F32) * zt).astype(BF16)

    def bstep(t, state):
        jc = nchunk - 1 - t
        c0 = pl.multiple_of(jc * ck, ck)
        sb_ref[jc] = state.astype(BF16)
        upd = jnp.dot(kz(c0, zt_b), v_ref[pl.ds(c0, ck), :], preferred_element_type=F32)
        return gc_b * state + upd

    lax.fori_loop(0, nchunk, bstep, jnp.zeros((RET_QK_DIM, RET_V_DIM), F32))

    def fstep(jc, state):
        c0 = pl.multiple_of(jc * ck, ck)
        rows = pl.ds(c0, ck)
        q = q_ref[rows, :]
        v = v_ref[rows, :]
        s = jnp.dot(q, kt_ref[:, pl.ds(c0, ck)], preferred_element_type=F32)
        out = jnp.dot((s * decay).astype(BF16), v, preferred_element_type=F32)
        qf = q.astype(F32)
        out = out + jnp.dot((qf * xi_f).astype(BF16), state.astype(BF16), preferred_element_type=F32)
        out = out + jnp.dot((qf * xi_b).astype(BF16), sb_ref[jc], preferred_element_type=F32)
        ms = jnp.mean(out * out, axis=-1, keepdims=True)
        y = out * lax.rsqrt(ms + EPS) * gn_ref[...]
        o_ref[rows, :] = (g_ref[rows, :].astype(F32) * y).astype(BF16)
        upd = jnp.dot(kz(c0, zt_f), v, preferred_element_type=F32)
        return gc_f * state + upd

    lax.fori_loop(0, nchunk, fstep, jnp.zeros((RET_QK_DIM, RET_V_DIM), F32))


def _retention(proj, dfw, dbw, gnorm, *, batch, seq, offs, ret_heads):
    off_q, off_k, off_v, off_g = offs
    ck = min(256, seq)
    vec = lambda w: pl.BlockSpec((None, 1, w), lambda b, h: (h, 0, 0))
    return pl.pallas_call(
        functools.partial(_retention_kernel, ck),
        grid=(batch, ret_heads),
        in_specs=[pl.BlockSpec((seq, RET_QK_DIM), lambda b, h: (b, off_q // RET_QK_DIM + h)),
                  pl.BlockSpec((seq, RET_QK_DIM), lambda b, h: (b, off_k // RET_QK_DIM + h)),
                  pl.BlockSpec((seq, RET_V_DIM), lambda b, h: (b, off_v // RET_V_DIM + h)),
                  pl.BlockSpec((seq, RET_V_DIM), lambda b, h: (b, off_g // RET_V_DIM + h)),
                  vec(LANES), vec(LANES), vec(RET_V_DIM)],
        out_specs=pl.BlockSpec((seq, RET_V_DIM), lambda b, h: (b, h)),
        out_shape=jax.ShapeDtypeStruct((batch * seq, ret_heads * RET_V_DIM), BF16),
        scratch_shapes=[pltpu.VMEM((RET_QK_DIM, seq), BF16),
                        pltpu.VMEM((seq // ck, RET_QK_DIM, RET_V_DIM), BF16)],
        compiler_params=_cparams("parallel", "parallel"),
        name="retention",
    )(proj, proj, proj, proj, dfw, dbw, gnorm)


def _outproj_kernel(emit_next, a_ref, r_ref, wa_ref, wr_ref, x_ref, *rest):
    if emit_next:
        g_ref, xo_ref, xg_ref, ssq_ref = rest
    else:
        (xo_ref,) = rest
    j = pl.program_id(1)
    acc = jnp.dot(a_ref[...], wa_ref[...], preferred_element_type=F32)
    acc = acc + jnp.dot(r_ref[...], wr_ref[...], preferred_element_type=F32)
    xn = x_ref[...] + acc
    xo_ref[...] = xn
    if emit_next:
        xg_ref[...] = (xn * g_ref[...]).astype(BF16)
        part = jnp.broadcast_to(jnp.sum(xn * xn, axis=-1, keepdims=True), ssq_ref.shape)

        @pl.when(j == 0)
        def _():
            ssq_ref[...] = part

        @pl.when(j > 0)
        def _():
            ssq_ref[...] = ssq_ref[...] + part


def _outproj(a, r, w, layer, x2, gnext):
    m, d = x2.shape
    ka, kr = a.shape[1], r.shape[1]
    assert ka == kr and w.shape[1] == ka + kr
    tm = min(1024, m)
    tn = min(512, d)
    emit_next = gnext is not None
    in_specs = [pl.BlockSpec((tm, ka), lambda i, j: (i, 0)),
                pl.BlockSpec((tm, kr), lambda i, j: (i, 0)),
                pl.BlockSpec((None, ka, tn), lambda i, j: (layer, 0, j)),
                pl.BlockSpec((None, kr, tn), lambda i, j: (layer, 1, j)),
                pl.BlockSpec((tm, tn), lambda i, j: (i, j))]
    out_specs = [pl.BlockSpec((tm, tn), lambda i, j: (i, j))]
    out_shape = [jax.ShapeDtypeStruct((m, d), F32)]
    args = [a, r, w, w, x2]
    if emit_next:
        in_specs.append(pl.BlockSpec((1, tn), lambda i, j: (0, j)))
        out_specs += [pl.BlockSpec((tm, tn), lambda i, j: (i, j)),
                      pl.BlockSpec((tm, LANES), lambda i, j: (i, 0))]
        out_shape += [jax.ShapeDtypeStruct((m, d), BF16),
                      jax.ShapeDtypeStruct((m, LANES), F32)]
        args.append(gnext)
    return pl.pallas_call(
        functools.partial(_outproj_kernel, emit_next),
        grid=(m // tm, d // tn),
        in_specs=in_specs,
        out_specs=out_specs,
        out_shape=out_shape,
        compiler_params=_cparams("parallel", "arbitrary"),
        name="outproj",
    )(*args)


def _final_kernel(x_ref, g_ref, o_ref):
    x = x_ref[...]
    ms = jnp.mean(x * x, axis=-1, keepdims=True)
    o_ref[...] = x * lax.rsqrt(ms + EPS) * g_ref[...]


def _final_norm(x2, g):
    m, d = x2.shape
    tm = min(256, m)
    return pl.pallas_call(
        _final_kernel,
        grid=(m // tm,),
        in_specs=[pl.BlockSpec((tm, d), lambda i: (i, 0)),
                  pl.BlockSpec((1, d), lambda i: (0, 0))],
        out_specs=pl.BlockSpec((tm, d), lambda i: (i, 0)),
        out_shape=jax.ShapeDtypeStruct((m, d), F32),
        compiler_params=_cparams("parallel"),
        name="final_norm",
    )(x2, g)


def _rope_tables(seq):
    rows = seq // GRID_W
    row = jnp.repeat(jnp.arange(rows), GRID_W).astype(F32)
    col = jnp.tile(jnp.arange(GRID_W), rows).astype(F32)
    axis_dim = HEAD_DIM // 2
    inv = ROPE_THETA ** (-jnp.arange(0, axis_dim, 2, dtype=F32) / axis_dim)
    ang_r = row[:, None] * inv[None, :]
    ang_c = col[:, None] * inv[None, :]
    cr, sr, cc, sc = jnp.cos(ang_r), jnp.sin(ang_r), jnp.cos(ang_c), jnp.sin(ang_c)
    z = jnp.zeros_like(sr)
    cos = jnp.concatenate([cr, cr, cc, cc], axis=-1)
    sina = jnp.concatenate([-sr, z, -sc, z], axis=-1)
    sinb = jnp.concatenate([z, sr, z, sc], axis=-1)
    return cos, sina, sinb


def kernel(x, norm_w, w_in, q_norm, k_norm, ret_decay_fwd, ret_decay_bwd, ret_norm, w_out, final_norm):
    batch, seq, d = x.shape
    depth = w_in.shape[0]
    attn_width = d // 2
    kv_width = attn_width // ATTN_GROUP
    ret_width = d - attn_width
    ret_heads = ret_width // RET_V_DIM
    ret_qk_width = ret_heads * RET_QK_DIM
    m = batch * seq

    widths = (("aq", attn_width), ("ak", kv_width), ("av", kv_width), ("gate", attn_width),
              ("rq", ret_qk_width), ("rk", ret_qk_width), ("rv", ret_width), ("gate", ret_width))
    tn = min(512, kv_width)
    sections, offs, pos = [], [], 0
    for kind, wdt in widths:
        assert wdt % tn == 0
        sections.append((kind if kind != "av" and kind != "rv" else "plain", pos // tn, (pos + wdt) // tn))
        offs.append(pos)
        pos += wdt
    assert pos == w_in.shape[2]
    ret_offs = tuple(offs[4:8])

    cos, sina, sinb = _rope_tables(seq)
    attn_scale = HEAD_DIM ** -0.5
    w_in_b = w_in.astype(BF16)
    w_out_b = w_out.astype(BF16)
    x2 = x.reshape(m, d)

    xg, ssq = _prologue(x2, norm_w[0].reshape(1, d))
    for l in range(depth):
        gq = (q_norm[l] * (attn_scale * LOG2E)).reshape(1, HEAD_DIM)
        gk = k_norm[l].reshape(1, HEAD_DIM)
        proj = _inproj(xg, w_in_b, l, ssq, cos, sina, sinb, gq, gk, seq=seq, sections=tuple(sections), tn=tn)
        a = _attention(proj, batch=batch, seq=seq, attn_width=attn_width, kv_width=kv_width)
        dfw = jnp.broadcast_to(ret_decay_fwd[l][:, None, None], (ret_heads, 1, LANES))
        dbw = jnp.broadcast_to(ret_decay_bwd[l][:, None, None], (ret_heads, 1, LANES))
        r = _retention(proj, dfw, dbw, ret_norm[l].reshape(ret_heads, 1, RET_V_DIM),
                       batch=batch, seq=seq, offs=ret_offs, ret_heads=ret_heads)
        if l + 1 < depth:
            x2, xg, ssq = _outproj(a, r, w_out_b, l, x2, norm_w[l + 1].reshape(1, d))
        else:
            (x2,) = _outproj(a, r, w_out_b, l, x2, None)
    out = _final_norm(x2, final_norm.reshape(1, d))
    return out.reshape(batch, seq, d)
```

```python
import functools
import math

import jax
import jax.numpy as jnp
from jax import lax
from jax.experimental import pallas as pl
from jax.experimental.pallas import tpu as pltpu

F32 = jnp.float32
BF16 = jnp.bfloat16

GRID_W = 64
HEAD_DIM = 128
ATTN_GROUP = 4
RET_V_DIM = 256
RET_QK_DIM = 128
ROPE_THETA = 10000.0
EPS = 1e-6
LOG2E = math.log2(math.e)

LANES = 128
VMEM_LIMIT = 56 * 1024 * 1024

NT_DIMS = (((1,), (1,)), ((), ()))


def _cparams(*sem):
    return pltpu.CompilerParams(dimension_semantics=sem, vmem_limit_bytes=VMEM_LIMIT)


def _prologue_kernel(x_ref, g_ref, xg_ref, ssq_ref):
    x = x_ref[...]
    xg_ref[...] = (x * g_ref[...]).astype(BF16)
    ssq_ref[...] = jnp.broadcast_to(jnp.sum(x * x, axis=-1, keepdims=True), ssq_ref.shape)


def _prologue(x2, g):
    m, d = x2.shape
    tm = min(256, m)
    return pl.pallas_call(
        _prologue_kernel,
        grid=(m // tm,),
        in_specs=[pl.BlockSpec((tm, d), lambda i: (i, 0)),
                  pl.BlockSpec((1, d), lambda i: (0, 0))],
        out_specs=[pl.BlockSpec((tm, d), lambda i: (i, 0)),
                   pl.BlockSpec((tm, LANES), lambda i: (i, 0))],
        out_shape=[jax.ShapeDtypeStruct((m, d), BF16),
                   jax.ShapeDtypeStruct((m, LANES), F32)],
        compiler_params=_cparams("parallel"),
        name="prologue",
    )(x2, g)


SECTION_ORDER = ("ag", "rg", "aq", "ak", "rq", "rk", "av", "rv")
SECTION_KIND = {"ag": "gate", "rg": "gate", "aq": "norm_rope", "ak": "norm_rope",
                "rq": "rope", "rk": "rope", "av": "plain", "rv": "plain"}
GAINED = ("aq", "ak", "rq", "rk")
ROW_CHUNK = 512


def _rope(y, cos, sin):
    return y * cos + pltpu.roll(y, HEAD_DIM // 2, 1) * sin


def _silu(y):
    h = 0.5 * y
    return h + h * jnp.tanh(h)


def _inproj_kernel(kind_ranges, n_tiles, d_model, xg_ref, w_ref, ssq_ref, cos_ref, sin_ref, gain_ref,
                   o_ref, acc_ref):
    j = pl.program_id(1)
    tm, tn = acc_ref.shape

    def matmul():
        acc_ref[...] = jnp.dot(xg_ref[...], w_ref[...], preferred_element_type=F32)

    def postprocess(kind):
        for c in range(tm // ROW_CHUNK):
            rows = slice(c * ROW_CHUNK, (c + 1) * ROW_CHUNK)
            r = lax.rsqrt(ssq_ref[rows, :] * (1.0 / d_model) + EPS)
            for g in range(tn // LANES):
                cols = slice(g * LANES, (g + 1) * LANES)
                y = acc_ref[rows, cols] * r
                if kind == "norm_rope":
                    ms = jnp.mean(y * y, axis=-1, keepdims=True)
                    y = _rope(y * lax.rsqrt(ms + EPS) * gain_ref[...], cos_ref[rows, :], sin_ref[rows, :])
                elif kind == "rope":
                    y = _rope(y * gain_ref[...], cos_ref[rows, :], sin_ref[rows, :])
                elif kind == "gate":
                    y = _silu(y)
                o_ref[rows, cols] = y.astype(BF16)

    @pl.when(j == 0)
    def _():
        matmul()

    for kind, lo, hi in kind_ranges:
        @pl.when(jnp.logical_and(j > lo, j <= min(hi, n_tiles - 1)))
        def _(kind=kind):
            postprocess(kind)
            matmul()

    @pl.when(j == n_tiles)
    def _():
        postprocess(kind_ranges[-1][0])


def _inproj(xg, w, layer, ssq, cos, sin, gains, *, seq, kind_ranges, gain_starts, tn):
    m, d = xg.shape
    n = w.shape[2]
    n_tiles = n // tn
    tm = min(1024, seq)
    sb = seq // tm
    assert kind_ranges[-1][2] == n_tiles and tm % ROW_CHUNK == 0

    def gain_index(i, j):
        t = j - 1
        return (sum((t >= s).astype(jnp.int32) for s in gain_starts[1:]), 0, 0)

    kern = functools.partial(_inproj_kernel, kind_ranges, n_tiles, d)
    tab = pl.BlockSpec((tm, LANES), lambda i, j: (i % sb, 0))
    return pl.pallas_call(
        kern,
        grid=(m // tm, n_tiles + 1),
        in_specs=[pl.BlockSpec((tm, d), lambda i, j: (i, 0)),
                  pl.BlockSpec((None, d, tn), lambda i, j: (layer, 0, jnp.minimum(j, n_tiles - 1))),
                  pl.BlockSpec((tm, LANES), lambda i, j: (i, 0)),
                  tab, tab,
                  pl.BlockSpec((None, 1, LANES), gain_index)],
        out_specs=pl.BlockSpec((tm, tn), lambda i, j: (i, jnp.maximum(j - 1, 0))),
        out_shape=jax.ShapeDtypeStruct((m, n), BF16),
        scratch_shapes=[pltpu.VMEM((tm, tn), F32)],
        compiler_params=_cparams("parallel", "arbitrary"),
        name="inproj",
    )(xg, w, ssq, cos, sin, gains)


def _transpose_via_mxu(a):
    rows = lax.broadcasted_iota(jnp.int32, (LANES, LANES), 0)
    cols = lax.broadcasted_iota(jnp.int32, (LANES, LANES), 1)
    eye = jnp.where(rows == cols, 1.0, 0.0).astype(BF16)
    return lax.dot_general(eye, a, NT_DIMS, preferred_element_type=F32).astype(BF16)


SAFE_SHIFT = 50.0
ONES_ROWS = 16


def _attn_kernel(tq, tk, q_ref, k_ref, v_ref, g_ref, o_ref, kaug_ref, vtaug_ref):
    seq = q_ref.shape[0]
    nq = ATTN_GROUP * tq
    k = k_ref[...]
    kaug_ref[:, :LANES] = k
    lane_k = lax.broadcasted_iota(jnp.int32, (seq, LANES), 1)
    kaug_ref[:, LANES:] = jnp.where(lane_k == 0, 1.0, 0.0).astype(BF16)
    vtaug_ref[:HEAD_DIM, :] = _transpose_via_mxu(v_ref[...])
    sub_v = lax.broadcasted_iota(jnp.int32, (ONES_ROWS, seq), 0)
    vtaug_ref[HEAD_DIM:, :] = jnp.where(sub_v == 0, 1.0, 0.0).astype(BF16)
    kf = k.astype(F32)
    kmax2 = jnp.max(jnp.sum(kf * kf, axis=1, keepdims=True), axis=0, keepdims=True)

    def qblock(qi, carry):
        rows = pl.ds(pl.multiple_of(qi * tq, tq), tq)
        qb = q_ref[rows, :]
        qs = jnp.concatenate([qb[:, g * LANES:(g + 1) * LANES] for g in range(ATTN_GROUP)], axis=0)
        qf = qs.astype(F32)
        bound = jnp.sqrt(jnp.sum(qf * qf, axis=1, keepdims=True) * kmax2)
        lane_q = lax.broadcasted_iota(jnp.int32, (nq, LANES), 1)
        qaug = jnp.concatenate([qs, jnp.where(lane_q == 0, -bound, 0.0).astype(BF16)], axis=1)

        def shifted():
            def kvstep(c, acc):
                c0 = pl.multiple_of(c * tk, tk)
                s = lax.dot_general(kaug_ref[pl.ds(c0, tk), :], qaug, NT_DIMS,
                                    preferred_element_type=F32)
                p = jnp.exp2(s).astype(BF16)
                return acc + jnp.dot(vtaug_ref[:, pl.ds(c0, tk)], p, preferred_element_type=F32)

            acc = lax.fori_loop(0, seq // tk, kvstep, jnp.zeros((HEAD_DIM + ONES_ROWS, nq), F32), unroll=True)
            return acc[:HEAD_DIM] * (1.0 / acc[HEAD_DIM:HEAD_DIM + 1])

        def running_max():
            def kvstep(c, st):
                m, l, acc = st
                c0 = pl.multiple_of(c * tk, tk)
                s = lax.dot_general(k_ref[pl.ds(c0, tk), :], qs, NT_DIMS,
                                    preferred_element_type=F32)
                m_new = jnp.maximum(m, jnp.max(s, axis=0, keepdims=True))
                alpha = jnp.exp2(m - m_new)
                p = jnp.exp2(s - m_new)
                l = alpha * l + jnp.sum(p, axis=0, keepdims=True)
                pv = jnp.dot(vtaug_ref[:HEAD_DIM, pl.ds(c0, tk)], p.astype(BF16), preferred_element_type=F32)
                return m_new, l, alpha * acc + pv

            init = (jnp.full((1, nq), -1e30, F32), jnp.zeros((1, nq), F32), jnp.zeros((HEAD_DIM, nq), F32))
            _, l, acc = lax.fori_loop(0, seq // tk, kvstep, init)
            return acc * (1.0 / l)

        o = lax.cond(jnp.max(bound) <= SAFE_SHIFT, shifted, running_max)
        for g in range(ATTN_GROUP):
            cols = slice(g * LANES, (g + 1) * LANES)
            og = o[:, g * tq:(g + 1) * tq].T
            o_ref[rows, cols] = (g_ref[rows, cols].astype(F32) * og).astype(BF16)
        return carry

    lax.fori_loop(0, seq // tq, qblock, 0)


def _attention(proj, *, batch, seq, attn_width, kv_width, offs):
    kvh = kv_width // HEAD_DIM
    gw = ATTN_GROUP * HEAD_DIM
    tq = min(256, seq)
    tk = min(512, seq)
    assert offs["aq"] % gw == 0 and offs["ag"] % gw == 0
    q_off = offs["aq"] // gw
    k_off = offs["ak"] // HEAD_DIM
    v_off = offs["av"] // HEAD_DIM
    g_off = offs["ag"] // gw
    return pl.pallas_call(
        functools.partial(_attn_kernel, tq, tk),
        grid=(batch, kvh),
        in_specs=[pl.BlockSpec((seq, gw), lambda b, h: (b, q_off + h)),
                  pl.BlockSpec((seq, HEAD_DIM), lambda b, h: (b, k_off + h)),
                  pl.BlockSpec((seq, HEAD_DIM), lambda b, h: (b, v_off + h)),
                  pl.BlockSpec((seq, gw), lambda b, h: (b, g_off + h))],
        out_specs=pl.BlockSpec((seq, gw), lambda b, h: (b, h)),
        out_shape=jax.ShapeDtypeStruct((batch * seq, attn_width), BF16),
        scratch_shapes=[pltpu.VMEM((seq, 2 * HEAD_DIM), BF16),
                        pltpu.VMEM((HEAD_DIM + ONES_ROWS, seq), BF16)],
        compiler_params=_cparams("parallel", "parallel"),
        name="attention",
    )(proj, proj, proj, proj)


def _log_sigmoid(x):
    return jnp.minimum(x, 0.0) - jnp.log1p(jnp.exp(-jnp.abs(x)))


def _retention_kernel(ck, q_ref, k_ref, v_ref, g_ref, df_ref, db_ref, gn_ref, o_ref, kt_ref, sb_ref):
    seq = q_ref.shape[0]
    nchunk = seq // ck
    lgf = _log_sigmoid(df_ref[...])[:, :1]
    lgb = _log_sigmoid(db_ref[...])[:, :1]
    kt_ref[...] = _transpose_via_mxu(k_ref[...])

    diff = (lax.broadcasted_iota(jnp.int32, (ck, ck), 0)
            - lax.broadcasted_iota(jnp.int32, (ck, ck), 1)).astype(F32)
    decay = jnp.exp(jnp.where(diff >= 0, lgf * diff, -lgb * diff))
    rowi = lax.broadcasted_iota(jnp.int32, (ck, RET_QK_DIM), 0).astype(F32)
    xi_f = jnp.exp(lgf * (rowi + 1.0))
    xi_b = jnp.exp(lgb * (ck - rowi))
    coli = lax.broadcasted_iota(jnp.int32, (RET_QK_DIM, ck), 1).astype(F32)
    zt_f = jnp.exp(lgf * (ck - 1.0 - coli))
    zt_b = jnp.exp(lgb * coli)
    gc_f = jnp.exp(lgf * ck)
    gc_b = jnp.exp(lgb * ck)

    def kz(c0, zt):
        return (kt_ref[:, pl.ds(c0, ck)].astype(F32) * zt).astype(BF16)

    def bstep(t, state):
        jc = nchunk - 1 - t
        c0 = pl.multiple_of(jc * ck, ck)
        sb_ref[jc] = state.astype(BF16)
        upd = jnp.dot(kz(c0, zt_b), v_ref[pl.ds(c0, ck), :], preferred_element_type=F32)
        return gc_b * state + upd

    lax.fori_loop(0, nchunk, bstep, jnp.zeros((RET_QK_DIM, RET_V_DIM), F32))

    def fstep(jc, state):
        c0 = pl.multiple_of(jc * ck, ck)
        rows = pl.ds(c0, ck)
        q = q_ref[rows, :]
        v = v_ref[rows, :]
        s = jnp.dot(q, kt_ref[:, pl.ds(c0, ck)], preferred_element_type=F32)
        out = jnp.dot((s * decay).astype(BF16), v, preferred_element_type=F32)
        qf = q.astype(F32)
        out = out + jnp.dot((qf * xi_f).astype(BF16), state.astype(BF16), preferred_element_type=F32)
        out = out + jnp.dot((qf * xi_b).astype(BF16), sb_ref[jc], preferred_element_type=F32)
        ms = jnp.mean(out * out, axis=-1, keepdims=True)
        y = out * lax.rsqrt(ms + EPS) * gn_ref[...]
        o_ref[rows, :] = (g_ref[rows, :].astype(F32) * y).astype(BF16)
        upd = jnp.dot(kz(c0, zt_f), v, preferred_element_type=F32)
        return gc_f * state + upd

    lax.fori_loop(0, nchunk, fstep, jnp.zeros((RET_QK_DIM, RET_V_DIM), F32))


def _retention(proj, dfw, dbw, gnorm, *, batch, seq, offs, ret_heads):
    off_q, off_k, off_v, off_g = offs["rq"], offs["rk"], offs["rv"], offs["rg"]
    assert off_v % RET_V_DIM == 0 and off_g % RET_V_DIM == 0
    ck = min(256, seq)
    vec = lambda w: pl.BlockSpec((None, 1, w), lambda b, h: (h, 0, 0))
    return pl.pallas_call(
        functools.partial(_retention_kernel, ck),
        grid=(batch, ret_heads),
        in_specs=[pl.BlockSpec((seq, RET_QK_DIM), lambda b, h: (b, off_q // RET_QK_DIM + h)),
                  pl.BlockSpec((seq, RET_QK_DIM), lambda b, h: (b, off_k // RET_QK_DIM + h)),
                  pl.BlockSpec((seq, RET_V_DIM), lambda b, h: (b, off_v // RET_V_DIM + h)),
                  pl.BlockSpec((seq, RET_V_DIM), lambda b, h: (b, off_g // RET_V_DIM + h)),
                  vec(LANES), vec(LANES), vec(RET_V_DIM)],
        out_specs=pl.BlockSpec((seq, RET_V_DIM), lambda b, h: (b, h)),
        out_shape=jax.ShapeDtypeStruct((batch * seq, ret_heads * RET_V_DIM), BF16),
        scratch_shapes=[pltpu.VMEM((RET_QK_DIM, seq), BF16),
                        pltpu.VMEM((seq // ck, RET_QK_DIM, RET_V_DIM), BF16)],
        compiler_params=_cparams("parallel", "parallel"),
        name="retention",
    )(proj, proj, proj, proj, dfw, dbw, gnorm)


def _outproj_kernel(emit_next, a_ref, r_ref, wa_ref, wr_ref, x_ref, *rest):
    if emit_next:
        g_ref, xo_ref, xg_ref, ssq_ref = rest
    else:
        (xo_ref,) = rest
    j = pl.program_id(1)
    acc = jnp.dot(a_ref[...], wa_ref[...], preferred_element_type=F32)
    acc = acc + jnp.dot(r_ref[...], wr_ref[...], preferred_element_type=F32)
    xn = x_ref[...] + acc
    xo_ref[...] = xn
    if emit_next:
        xg_ref[...] = (xn * g_ref[...]).astype(BF16)
        part = jnp.broadcast_to(jnp.sum(xn * xn, axis=-1, keepdims=True), ssq_ref.shape)

        @pl.when(j == 0)
        def _():
            ssq_ref[...] = part

        @pl.when(j > 0)
        def _():
            ssq_ref[...] = ssq_ref[...] + part


def _outproj(a, r, w, layer, x2, gnext):
    m, d = x2.shape
    ka, kr = a.shape[1], r.shape[1]
    assert ka == kr and w.shape[1] == ka + kr
    tm = min(1024, m)
    tn = min(512, d)
    emit_next = gnext is not None
    in_specs = [pl.BlockSpec((tm, ka), lambda i, j: (i, 0)),
                pl.BlockSpec((tm, kr), lambda i, j: (i, 0)),
                pl.BlockSpec((None, ka, tn), lambda i, j: (layer, 0, j)),
                pl.BlockSpec((None, kr, tn), lambda i, j: (layer, 1, j)),
                pl.BlockSpec((tm, tn), lambda i, j: (i, j))]
    out_specs = [pl.BlockSpec((tm, tn), lambda i, j: (i, j))]
    out_shape = [jax.ShapeDtypeStruct((m, d), F32)]
    args = [a, r, w, w, x2]
    if emit_next:
        in_specs.append(pl.BlockSpec((1, tn), lambda i, j: (0, j)))
        out_specs += [pl.BlockSpec((tm, tn), lambda i, j: (i, j)),
                      pl.BlockSpec((tm, LANES), lambda i, j: (i, 0))]
        out_shape += [jax.ShapeDtypeStruct((m, d), BF16),
                      jax.ShapeDtypeStruct((m, LANES), F32)]
        args.append(gnext)
    return pl.pallas_call(
        functools.partial(_outproj_kernel, emit_next),
        grid=(m // tm, d // tn),
        in_specs=in_specs,
        out_specs=out_specs,
        out_shape=out_shape,
        compiler_params=_cparams("parallel", "arbitrary"),
        name="outproj",
    )(*args)


def _final_kernel(x_ref, g_ref, o_ref):
    x = x_ref[...]
    ms = jnp.mean(x * x, axis=-1, keepdims=True)
    o_ref[...] = x * lax.rsqrt(ms + EPS) * g_ref[...]


def _final_norm(x2, g):
    m, d = x2.shape
    tm = min(256, m)
    return pl.pallas_call(
        _final_kernel,
        grid=(m // tm,),
        in_specs=[pl.BlockSpec((tm, d), lambda i: (i, 0)),
                  pl.BlockSpec((1, d), lambda i: (0, 0))],
        out_specs=pl.BlockSpec((tm, d), lambda i: (i, 0)),
        out_shape=jax.ShapeDtypeStruct((m, d), F32),
        compiler_params=_cparams("parallel"),
        name="final_norm",
    )(x2, g)


def _rope_tables(seq):
    rows = seq // GRID_W
    row = jnp.repeat(jnp.arange(rows), GRID_W).astype(F32)
    col = jnp.tile(jnp.arange(GRID_W), rows).astype(F32)
    axis_dim = HEAD_DIM // 2
    inv = ROPE_THETA ** (-jnp.arange(0, axis_dim, 2, dtype=F32) / axis_dim)
    ang_r = row[:, None] * inv[None, :]
    ang_c = col[:, None] * inv[None, :]
    cr, sr, cc, sc = jnp.cos(ang_r), jnp.sin(ang_r), jnp.cos(ang_c), jnp.sin(ang_c)
    cos = jnp.concatenate([cr, cc, cr, cc], axis=-1)
    sin = jnp.concatenate([-sr, -sc, sr, sc], axis=-1)
    return cos, sin


def _head_layout(a):
    lead, width = a.shape[:-1], a.shape[-1]
    a = a.reshape(*lead, width // HEAD_DIM, 2, 2, HEAD_DIM // 4)
    return jnp.swapaxes(a, -3, -2).reshape(*lead, width)


def kernel(x, norm_w, w_in, q_norm, k_norm, ret_decay_fwd, ret_decay_bwd, ret_norm, w_out, final_norm):
    batch, seq, d = x.shape
    depth = w_in.shape[0]
    attn_width = d // 2
    kv_width = attn_width // ATTN_GROUP
    ret_width = d - attn_width
    ret_heads = ret_width // RET_V_DIM
    ret_qk_width = ret_heads * RET_QK_DIM
    m = batch * seq

    ref_widths = (("aq", attn_width), ("ak", kv_width), ("av", kv_width), ("ag", attn_width),
                  ("rq", ret_qk_width), ("rk", ret_qk_width), ("rv", ret_width), ("rg", ret_width))
    ref_pos, pos = {}, 0
    for name, wdt in ref_widths:
        ref_pos[name] = (pos, wdt)
        pos += wdt
    assert pos == w_in.shape[2]
    tn = min(512, kv_width)
    offs, kind_ranges, gain_starts, pos = {}, [], [], 0
    for name in SECTION_ORDER:
        wdt = ref_pos[name][1]
        assert wdt % tn == 0
        offs[name] = pos
        kind = SECTION_KIND[name]
        if kind_ranges and kind_ranges[-1][0] == kind:
            kind_ranges[-1] = (kind, kind_ranges[-1][1], (pos + wdt) // tn)
        else:
            kind_ranges.append((kind, pos // tn, (pos + wdt) // tn))
        if name in GAINED:
            gain_starts.append(pos // tn)
        pos += wdt

    def section(name):
        lo, wdt = ref_pos[name]
        cols = w_in[:, :, lo:lo + wdt]
        return _head_layout(cols) if name in GAINED else cols

    w_in_b = jnp.concatenate([section(name) for name in SECTION_ORDER], axis=-1).astype(BF16)
    w_out_b = w_out.astype(BF16)
    cos, sin = _rope_tables(seq)
    attn_scale = HEAD_DIM ** -0.5
    x2 = x.reshape(m, d)

    xg, ssq = _prologue(x2, norm_w[0].reshape(1, d))
    for l in range(depth):
        gains = jnp.stack([_head_layout(q_norm[l]) * (attn_scale * LOG2E),
                           _head_layout(k_norm[l]),
                           jnp.ones((HEAD_DIM,), F32),
                           jnp.full((HEAD_DIM,), RET_QK_DIM ** -0.5, F32)]).reshape(len(GAINED), 1, HEAD_DIM)
        proj = _inproj(xg, w_in_b, l, ssq, cos, sin, gains, seq=seq, kind_ranges=tuple(kind_ranges),
                       gain_starts=tuple(gain_starts), tn=tn)
        a = _attention(proj, batch=batch, seq=seq, attn_width=attn_width, kv_width=kv_width, offs=offs)
        dfw = jnp.broadcast_to(ret_decay_fwd[l][:, None, None], (ret_heads, 1, LANES))
        dbw = jnp.broadcast_to(ret_decay_bwd[l][:, None, None], (ret_heads, 1, LANES))
        r = _retention(proj, dfw, dbw, ret_norm[l].reshape(ret_heads, 1, RET_V_DIM),
                       batch=batch, seq=seq, offs=offs, ret_heads=ret_heads)
        if l + 1 < depth:
            x2, xg, ssq = _outproj(a, r, w_out_b, l, x2, norm_w[l + 1].reshape(1, d))
        else:
            (x2,) = _outproj(a, r, w_out_b, l, x2, None)
    out = _final_norm(x2, final_norm.reshape(1, d))
    return out.reshape(batch, seq, d)
```

```python
import functools
import math

import jax
import jax.numpy as jnp
from jax import lax
from jax.experimental import pallas as pl
from jax.experimental.pallas import tpu as pltpu

F32 = jnp.float32
BF16 = jnp.bfloat16

GRID_W = 64
HEAD_DIM = 128
ATTN_GROUP = 4
RET_V_DIM = 256
RET_QK_DIM = 128
ROPE_THETA = 10000.0
EPS = 1e-6
LOG2E = math.log2(math.e)

LANES = 128
VMEM_LIMIT = 56 * 1024 * 1024

NT_DIMS = (((1,), (1,)), ((), ()))


def _cparams(*sem):
    return pltpu.CompilerParams(dimension_semantics=sem, vmem_limit_bytes=VMEM_LIMIT)


def _prologue_kernel(x_ref, g_ref, xg_ref, ssq_ref):
    x = x_ref[...]
    xg_ref[...] = (x * g_ref[...]).astype(BF16)
    ssq_ref[...] = jnp.broadcast_to(jnp.sum(x * x, axis=-1, keepdims=True), ssq_ref.shape)


def _prologue(x2, g):
    m, d = x2.shape
    tm = min(256, m)
    return pl.pallas_call(
        _prologue_kernel,
        grid=(m // tm,),
        in_specs=[pl.BlockSpec((tm, d), lambda i: (i, 0)),
                  pl.BlockSpec((1, d), lambda i: (0, 0))],
        out_specs=[pl.BlockSpec((tm, d), lambda i: (i, 0)),
                   pl.BlockSpec((tm, LANES), lambda i: (i, 0))],
        out_shape=[jax.ShapeDtypeStruct((m, d), BF16),
                   jax.ShapeDtypeStruct((m, LANES), F32)],
        compiler_params=_cparams("parallel"),
        name="prologue",
    )(x2, g)


SECTION_ORDER = ("ag", "rg", "aq", "ak", "rq", "rk", "av", "rv")
SECTION_KIND = {"ag": "gate", "rg": "gate", "aq": "norm_rope", "ak": "norm_rope",
                "rq": "rope", "rk": "rope", "av": "plain", "rv": "plain"}
GAINED = ("aq", "ak", "rq", "rk")
ROW_CHUNK = 512


def _rope(y, cos, sin_up, sin_dn):
    q = HEAD_DIM // 4
    return y * cos + pltpu.roll(y, HEAD_DIM - q, 1) * sin_up + pltpu.roll(y, q, 1) * sin_dn


def _silu(y):
    h = 0.5 * y
    return h + h * jnp.tanh(h)


def _inproj_kernel(kind_ranges, n_tiles, d_model, xg_ref, w_ref, ssq_ref, cos_ref, sup_ref, sdn_ref, gain_ref,
                   o_ref, acc_ref):
    j = pl.program_id(1)
    tm, tn = acc_ref.shape

    def matmul():
        acc_ref[...] = jnp.dot(xg_ref[...], w_ref[...], preferred_element_type=F32)

    def postprocess(kind):
        for c in range(tm // ROW_CHUNK):
            rows = slice(c * ROW_CHUNK, (c + 1) * ROW_CHUNK)
            r = lax.rsqrt(ssq_ref[rows, :] * (1.0 / d_model) + EPS)
            for g in range(tn // LANES):
                cols = slice(g * LANES, (g + 1) * LANES)
                y = acc_ref[rows, cols] * r
                if kind == "norm_rope":
                    ms = jnp.mean(y * y, axis=-1, keepdims=True)
                    y = _rope(y * lax.rsqrt(ms + EPS) * gain_ref[...],
                              cos_ref[rows, :], sup_ref[rows, :], sdn_ref[rows, :])
                elif kind == "rope":
                    y = _rope(y * gain_ref[...], cos_ref[rows, :], sup_ref[rows, :], sdn_ref[rows, :])
                elif kind == "gate":
                    y = _silu(y)
                o_ref[rows, cols] = y.astype(BF16)

    @pl.when(j == 0)
    def _():
        matmul()

    for kind, lo, hi in kind_ranges:
        @pl.when(jnp.logical_and(j > lo, j <= min(hi, n_tiles - 1)))
        def _(kind=kind):
            postprocess(kind)
            matmul()

    @pl.when(j == n_tiles)
    def _():
        postprocess(kind_ranges[-1][0])


def _inproj(xg, w, layer, ssq, tables, gains, *, seq, kind_ranges, gain_starts, tile_shifts, tn):
    m, d = xg.shape
    n = w.shape[2]
    n_tiles = n // tn
    tm = min(1024, seq)
    sb = seq // tm
    assert kind_ranges[-1][2] == n_tiles and tm % ROW_CHUNK == 0

    def gain_index(i, j):
        t = j - 1
        return (sum((t >= s).astype(jnp.int32) for s in gain_starts[1:]), 0, 0)

    def weight_index(i, j):
        t = jnp.minimum(j, n_tiles - 1)
        shift = sum(jnp.where(jnp.logical_and(t >= lo, t < hi), sh, 0) for lo, hi, sh in tile_shifts)
        return (layer, 0, t + shift)

    kern = functools.partial(_inproj_kernel, kind_ranges, n_tiles, d)
    tab = pl.BlockSpec((tm, LANES), lambda i, j: (i % sb, 0))
    return pl.pallas_call(
        kern,
        grid=(m // tm, n_tiles + 1),
        in_specs=[pl.BlockSpec((tm, d), lambda i, j: (i, 0)),
                  pl.BlockSpec((None, d, tn), weight_index),
                  pl.BlockSpec((tm, LANES), lambda i, j: (i, 0)),
                  tab, tab, tab,
                  pl.BlockSpec((None, 1, LANES), gain_index)],
        out_specs=pl.BlockSpec((tm, tn), lambda i, j: (i, jnp.maximum(j - 1, 0))),
        out_shape=jax.ShapeDtypeStruct((m, n), BF16),
        scratch_shapes=[pltpu.VMEM((tm, tn), F32)],
        compiler_params=_cparams("parallel", "arbitrary"),
        name="inproj",
    )(xg, w, ssq, *tables, gains)


def _transpose_via_mxu(a):
    rows = lax.broadcasted_iota(jnp.int32, (LANES, LANES), 0)
    cols = lax.broadcasted_iota(jnp.int32, (LANES, LANES), 1)
    eye = jnp.where(rows == cols, 1.0, 0.0).astype(BF16)
    return lax.dot_general(eye, a, NT_DIMS, preferred_element_type=F32).astype(BF16)


SCORE_LIMIT = 50.0
Q_UNROLL = 2


def _max_sq_norm(a):
    af = a.astype(F32)
    ones = jnp.ones((8, LANES), BF16)
    norms = lax.dot_general(ones, (af * af).astype(BF16), NT_DIMS, preferred_element_type=F32)
    return jnp.max(norms)


def _attn_kernel(tq, tk, q_ref, k_ref, v_ref, g_ref, o_ref, vt_ref):
    seq = q_ref.shape[0]
    nq = ATTN_GROUP * tq
    vt_ref[...] = _transpose_via_mxu(v_ref[...])
    qmax2 = _max_sq_norm(q_ref[:, :LANES])
    for g in range(1, ATTN_GROUP):
        qmax2 = jnp.maximum(qmax2, _max_sq_norm(q_ref[:, g * LANES:(g + 1) * LANES]))
    bound2 = qmax2 * _max_sq_norm(k_ref[...]) * 1.02

    def load_q(qi):
        rows = pl.ds(pl.multiple_of(qi * tq, tq), tq)
        qb = q_ref[rows, :]
        qs = jnp.concatenate([qb[:, g * LANES:(g + 1) * LANES] for g in range(ATTN_GROUP)], axis=0)
        return rows, qs

    def finish(rows, o):
        for g in range(ATTN_GROUP):
            cols = slice(g * LANES, (g + 1) * LANES)
            og = o[:, g * tq:(g + 1) * tq].T
            o_ref[rows, cols] = (g_ref[rows, cols].astype(F32) * og).astype(BF16)

    def unshifted():
        def qblock(qi, carry):
            rows, qs = load_q(qi)

            def kvstep(c, st):
                l, acc = st
                c0 = pl.multiple_of(c * tk, tk)
                s = lax.dot_general(k_ref[pl.ds(c0, tk), :], qs, NT_DIMS,
                                    preferred_element_type=F32)
                p = jnp.exp2(s)
                l = l + jnp.sum(p.reshape(tk // 8, 8, nq), axis=0)
                return l, acc + jnp.dot(vt_ref[:, pl.ds(c0, tk)], p.astype(BF16), preferred_element_type=F32)

            init = (jnp.zeros((8, nq), F32), jnp.zeros((HEAD_DIM, nq), F32))
            l, acc = lax.fori_loop(0, seq // tk, kvstep, init, unroll=True)
            finish(rows, acc * (1.0 / jnp.sum(l, axis=0, keepdims=True)))
            return carry

        lax.fori_loop(0, seq // tq, qblock, 0, unroll=Q_UNROLL)

    def running_max():
        def qblock(qi, carry):
            rows, qs = load_q(qi)

            def kvstep(c, st):
                m, l, acc = st
                c0 = pl.multiple_of(c * tk, tk)
                s = lax.dot_general(k_ref[pl.ds(c0, tk), :], qs, NT_DIMS,
                                    preferred_element_type=F32)
                m_new = jnp.maximum(m, jnp.max(s, axis=0, keepdims=True))
                alpha = jnp.exp2(m - m_new)
                p = jnp.exp2(s - m_new)
                l = alpha * l + jnp.sum(p, axis=0, keepdims=True)
                pv = jnp.dot(vt_ref[:, pl.ds(c0, tk)], p.astype(BF16), preferred_element_type=F32)
                return m_new, l, alpha * acc + pv

            init = (jnp.full((1, nq), -1e30, F32), jnp.zeros((1, nq), F32), jnp.zeros((HEAD_DIM, nq), F32))
            _, l, acc = lax.fori_loop(0, seq // tk, kvstep, init)
            finish(rows, acc * (1.0 / l))
            return carry

        lax.fori_loop(0, seq // tq, qblock, 0)

    lax.cond(bound2 <= SCORE_LIMIT * SCORE_LIMIT, unshifted, running_max)


def _attention(proj, *, batch, seq, attn_width, kv_width, offs):
    kvh = kv_width // HEAD_DIM
    gw = ATTN_GROUP * HEAD_DIM
    tq = min(256, seq)
    tk = min(512, seq)
    assert offs["aq"] % gw == 0 and offs["ag"] % gw == 0
    q_off = offs["aq"] // gw
    k_off = offs["ak"] // HEAD_DIM
    v_off = offs["av"] // HEAD_DIM
    g_off = offs["ag"] // gw
    return pl.pallas_call(
        functools.partial(_attn_kernel, tq, tk),
        grid=(batch, kvh),
        in_specs=[pl.BlockSpec((seq, gw), lambda b, h: (b, q_off + h)),
                  pl.BlockSpec((seq, HEAD_DIM), lambda b, h: (b, k_off + h)),
                  pl.BlockSpec((seq, HEAD_DIM), lambda b, h: (b, v_off + h)),
                  pl.BlockSpec((seq, gw), lambda b, h: (b, g_off + h))],
        out_specs=pl.BlockSpec((seq, gw), lambda b, h: (b, h)),
        out_shape=jax.ShapeDtypeStruct((batch * seq, attn_width), BF16),
        scratch_shapes=[pltpu.VMEM((HEAD_DIM, seq), BF16)],
        compiler_params=_cparams("parallel", "parallel"),
        name="attention",
    )(proj, proj, proj, proj)


RET_UNROLL = 8


def _log_sigmoid(x):
    return jnp.minimum(x, 0.0) - jnp.log1p(jnp.exp(-jnp.abs(x)))


def _retention_kernel(ck, q_ref, k_ref, v_ref, g_ref, df_ref, db_ref, gn_ref, o_ref, kt_ref, sb_ref):
    seq = q_ref.shape[0]
    nchunk = seq // ck
    lgf = _log_sigmoid(df_ref[...])[:, :1]
    lgb = _log_sigmoid(db_ref[...])[:, :1]
    kt_ref[...] = _transpose_via_mxu(k_ref[...])

    diff = (lax.broadcasted_iota(jnp.int32, (ck, ck), 0)
            - lax.broadcasted_iota(jnp.int32, (ck, ck), 1)).astype(F32)
    decay = jnp.exp(jnp.where(diff >= 0, lgf * diff, -lgb * diff))
    rowi = lax.broadcasted_iota(jnp.int32, (ck, RET_QK_DIM), 0).astype(F32)
    xi_f = jnp.exp(lgf * (rowi + 1.0))
    xi_b = jnp.exp(lgb * (ck - rowi))
    coli = lax.broadcasted_iota(jnp.int32, (RET_QK_DIM, ck), 1).astype(F32)
    zt_f = jnp.exp(lgf * (ck - 1.0 - coli))
    zt_b = jnp.exp(lgb * coli)
    gc_f = jnp.exp(lgf * ck)
    gc_b = jnp.exp(lgb * ck)

    def kz(c0, zt):
        return (kt_ref[:, pl.ds(c0, ck)].astype(F32) * zt).astype(BF16)

    def bstep(t, state):
        jc = nchunk - 1 - t
        c0 = pl.multiple_of(jc * ck, ck)
        sb_ref[jc] = state.astype(BF16)
        upd = jnp.dot(kz(c0, zt_b), v_ref[pl.ds(c0, ck), :], preferred_element_type=F32)
        return gc_b * state + upd

    lax.fori_loop(0, nchunk, bstep, jnp.zeros((RET_QK_DIM, RET_V_DIM), F32), unroll=RET_UNROLL)

    def fstep(jc, state):
        c0 = pl.multiple_of(jc * ck, ck)
        rows = pl.ds(c0, ck)
        q = q_ref[rows, :]
        v = v_ref[rows, :]
        s = jnp.dot(q, kt_ref[:, pl.ds(c0, ck)], preferred_element_type=F32)
        out = jnp.dot((s * decay).astype(BF16), v, preferred_element_type=F32)
        qf = q.astype(F32)
        qx = jnp.concatenate([(qf * xi_f).astype(BF16), (qf * xi_b).astype(BF16)], axis=1)
        both = jnp.concatenate([state.astype(BF16), sb_ref[jc]], axis=0)
        out = out + jnp.dot(qx, both, preferred_element_type=F32)
        ms = jnp.mean(out * out, axis=-1, keepdims=True)
        y = out * lax.rsqrt(ms + EPS) * gn_ref[...]
        o_ref[rows, :] = (g_ref[rows, :].astype(F32) * y).astype(BF16)
        upd = jnp.dot(kz(c0, zt_f), v, preferred_element_type=F32)
        return gc_f * state + upd

    lax.fori_loop(0, nchunk, fstep, jnp.zeros((RET_QK_DIM, RET_V_DIM), F32), unroll=RET_UNROLL)


def _retention(proj, dfw, dbw, gnorm, *, batch, seq, offs, ret_heads):
    off_q, off_k, off_v, off_g = offs["rq"], offs["rk"], offs["rv"], offs["rg"]
    assert off_v % RET_V_DIM == 0 and off_g % RET_V_DIM == 0
    ck = min(256, seq)
    vec = lambda w: pl.BlockSpec((None, 1, w), lambda b, h: (h, 0, 0))
    return pl.pallas_call(
        functools.partial(_retention_kernel, ck),
        grid=(batch, ret_heads),
        in_specs=[pl.BlockSpec((seq, RET_QK_DIM), lambda b, h: (b, off_q // RET_QK_DIM + h)),
                  pl.BlockSpec((seq, RET_QK_DIM), lambda b, h: (b, off_k // RET_QK_DIM + h)),
                  pl.BlockSpec((seq, RET_V_DIM), lambda b, h: (b, off_v // RET_V_DIM + h)),
                  pl.BlockSpec((seq, RET_V_DIM), lambda b, h: (b, off_g // RET_V_DIM + h)),
                  vec(LANES), vec(LANES), vec(RET_V_DIM)],
        out_specs=pl.BlockSpec((seq, RET_V_DIM), lambda b, h: (b, h)),
        out_shape=jax.ShapeDtypeStruct((batch * seq, ret_heads * RET_V_DIM), BF16),
        scratch_shapes=[pltpu.VMEM((RET_QK_DIM, seq), BF16),
                        pltpu.VMEM((seq // ck, RET_QK_DIM, RET_V_DIM), BF16)],
        compiler_params=_cparams("parallel", "parallel"),
        name="retention",
    )(proj, proj, proj, proj, dfw, dbw, gnorm)


def _outproj_kernel(emit_next, a_ref, r_ref, wa_ref, wr_ref, x_ref, *rest):
    if emit_next:
        g_ref, xo_ref, xg_ref, ssq_ref = rest
    else:
        (xo_ref,) = rest
    j = pl.program_id(1)
    acc = jnp.dot(a_ref[...], wa_ref[...], preferred_element_type=F32)
    acc = acc + jnp.dot(r_ref[...], wr_ref[...], preferred_element_type=F32)
    xn = x_ref[...] + acc
    xo_ref[...] = xn
    if emit_next:
        xg_ref[...] = (xn * g_ref[...]).astype(BF16)
        part = jnp.broadcast_to(jnp.sum(xn * xn, axis=-1, keepdims=True), ssq_ref.shape)

        @pl.when(j == 0)
        def _():
            ssq_ref[...] = part

        @pl.when(j > 0)
        def _():
            ssq_ref[...] = ssq_ref[...] + part


def _outproj(a, r, w, layer, x2, gnext):
    m, d = x2.shape
    ka, kr = a.shape[1], r.shape[1]
    assert ka == kr and w.shape[1] == ka + kr
    tm = min(1024, m)
    tn = min(512, d)
    emit_next = gnext is not None
    in_specs = [pl.BlockSpec((tm, ka), lambda i, j: (i, 0)),
                pl.BlockSpec((tm, kr), lambda i, j: (i, 0)),
                pl.BlockSpec((None, ka, tn), lambda i, j: (layer, 0, j)),
                pl.BlockSpec((None, kr, tn), lambda i, j: (layer, 1, j)),
                pl.BlockSpec((tm, tn), lambda i, j: (i, j))]
    out_specs = [pl.BlockSpec((tm, tn), lambda i, j: (i, j))]
    out_shape = [jax.ShapeDtypeStruct((m, d), F32)]
    args = [a, r, w, w, x2]
    if emit_next:
        in_specs.append(pl.BlockSpec((1, tn), lambda i, j: (0, j)))
        out_specs += [pl.BlockSpec((tm, tn), lambda i, j: (i, j)),
                      pl.BlockSpec((tm, LANES), lambda i, j: (i, 0))]
        out_shape += [jax.ShapeDtypeStruct((m, d), BF16),
                      jax.ShapeDtypeStruct((m, LANES), F32)]
        args.append(gnext)
    return pl.pallas_call(
        functools.partial(_outproj_kernel, emit_next),
        grid=(m // tm, d // tn),
        in_specs=in_specs,
        out_specs=out_specs,
        out_shape=out_shape,
        compiler_params=_cparams("parallel", "arbitrary"),
        name="outproj",
    )(*args)


def _final_kernel(x_ref, g_ref, o_ref):
    x = x_ref[...]
    ms = jnp.mean(x * x, axis=-1, keepdims=True)
    o_ref[...] = x * lax.rsqrt(ms + EPS) * g_ref[...]


def _final_norm(x2, g):
    m, d = x2.shape
    tm = min(256, m)
    return pl.pallas_call(
        _final_kernel,
        grid=(m // tm,),
        in_specs=[pl.BlockSpec((tm, d), lambda i: (i, 0)),
                  pl.BlockSpec((1, d), lambda i: (0, 0))],
        out_specs=pl.BlockSpec((tm, d), lambda i: (i, 0)),
        out_shape=jax.ShapeDtypeStruct((m, d), F32),
        compiler_params=_cparams("parallel"),
        name="final_norm",
    )(x2, g)


def _rope_tables(seq):
    rows = seq // GRID_W
    row = jnp.repeat(jnp.arange(rows), GRID_W).astype(F32)
    col = jnp.tile(jnp.arange(GRID_W), rows).astype(F32)
    axis_dim = HEAD_DIM // 2
    inv = ROPE_THETA ** (-jnp.arange(0, axis_dim, 2, dtype=F32) / axis_dim)
    ang_r = row[:, None] * inv[None, :]
    ang_c = col[:, None] * inv[None, :]
    cr, sr, cc, sc = jnp.cos(ang_r), jnp.sin(ang_r), jnp.cos(ang_c), jnp.sin(ang_c)
    z = jnp.zeros_like(sr)
    cos = jnp.concatenate([cr, cr, cc, cc], axis=-1)
    sin_up = jnp.concatenate([-sr, z, -sc, z], axis=-1)
    sin_dn = jnp.concatenate([z, sr, z, sc], axis=-1)
    return cos, sin_up, sin_dn


def kernel(x, norm_w, w_in, q_norm, k_norm, ret_decay_fwd, ret_decay_bwd, ret_norm, w_out, final_norm):
    batch, seq, d = x.shape
    depth = w_in.shape[0]
    attn_width = d // 2
    kv_width = attn_width // ATTN_GROUP
    ret_width = d - attn_width
    ret_heads = ret_width // RET_V_DIM
    ret_qk_width = ret_heads * RET_QK_DIM
    m = batch * seq

    ref_widths = (("aq", attn_width), ("ak", kv_width), ("av", kv_width), ("ag", attn_width),
                  ("rq", ret_qk_width), ("rk", ret_qk_width), ("rv", ret_width), ("rg", ret_width))
    ref_pos, pos = {}, 0
    for name, wdt in ref_widths:
        ref_pos[name] = (pos, wdt)
        pos += wdt
    assert pos == w_in.shape[2]
    tn = min(512, kv_width)
    offs, kind_ranges, gain_starts, pos = {}, [], [], 0
    for name in SECTION_ORDER:
        wdt = ref_pos[name][1]
        assert wdt % tn == 0
        offs[name] = pos
        kind = SECTION_KIND[name]
        if kind_ranges and kind_ranges[-1][0] == kind:
            kind_ranges[-1] = (kind, kind_ranges[-1][1], (pos + wdt) // tn)
        else:
            kind_ranges.append((kind, pos // tn, (pos + wdt) // tn))
        if name in GAINED:
            gain_starts.append(pos // tn)
        pos += wdt

    tile_shifts = tuple((offs[name] // tn, (offs[name] + ref_pos[name][1]) // tn, (ref_pos[name][0] - offs[name]) // tn)
                        for name in SECTION_ORDER)
    w_in_b = w_in.astype(BF16)
    w_out_b = w_out.astype(BF16)
    tables = _rope_tables(seq)
    attn_scale = HEAD_DIM ** -0.5
    x2 = x.reshape(m, d)

    xg, ssq = _prologue(x2, norm_w[0].reshape(1, d))
    for l in range(depth):
        gains = jnp.stack([q_norm[l] * (attn_scale * LOG2E),
                           k_norm[l],
                           jnp.ones((HEAD_DIM,), F32),
                           jnp.full((HEAD_DIM,), RET_QK_DIM ** -0.5, F32)]).reshape(len(GAINED), 1, HEAD_DIM)
        proj = _inproj(xg, w_in_b, l, ssq, tables, gains, seq=seq, kind_ranges=tuple(kind_ranges),
                       gain_starts=tuple(gain_starts), tile_shifts=tile_shifts, tn=tn)
        a = _attention(proj, batch=batch, seq=seq, attn_width=attn_width, kv_width=kv_width, offs=offs)
        dfw = jnp.broadcast_to(ret_decay_fwd[l][:, None, None], (ret_heads, 1, LANES))
        dbw = jnp.broadcast_to(ret_decay_bwd[l][:, None, None], (ret_heads, 1, LANES))
        r = _retention(proj, dfw, dbw, ret_norm[l].reshape(ret_heads, 1, RET_V_DIM),
                       batch=batch, seq=seq, offs=offs, ret_heads=ret_heads)
        if l + 1 < depth:
            x2, xg, ssq = _outproj(a, r, w_out_b, l, x2, norm_w[l + 1].reshape(1, d))
        else:
            (x2,) = _outproj(a, r, w_out_b, l, x2, None)
    out = _final_norm(x2, final_norm.reshape(1, d))
    return out.reshape(batch, seq, d)
```

```python
import functools
import math

import jax
import jax.numpy as jnp
from jax import lax
from jax.experimental import pallas as pl
from jax.experimental.pallas import tpu as pltpu

F32 = jnp.float32
BF16 = jnp.bfloat16

GRID_W = 64
HEAD_DIM = 128
ATTN_GROUP = 4
RET_V_DIM = 256
RET_QK_DIM = 128
ROPE_THETA = 10000.0
EPS = 1e-6
LOG2E = math.log2(math.e)

LANES = 128
VMEM_LIMIT = 56 * 1024 * 1024

NT_DIMS = (((1,), (1,)), ((), ()))


def _cparams(*sem):
    return pltpu.CompilerParams(dimension_semantics=sem, vmem_limit_bytes=VMEM_LIMIT)


def _prologue_kernel(x_ref, g_ref, xg_ref, ssq_ref):
    x = x_ref[...]
    xg_ref[...] = (x * g_ref[...]).astype(BF16)
    ssq_ref[...] = jnp.broadcast_to(jnp.sum(x * x, axis=-1, keepdims=True), ssq_ref.shape)


def _prologue(x2, g):
    m, d = x2.shape
    tm = min(256, m)
    return pl.pallas_call(
        _prologue_kernel,
        grid=(m // tm,),
        in_specs=[pl.BlockSpec((tm, d), lambda i: (i, 0)),
                  pl.BlockSpec((1, d), lambda i: (0, 0))],
        out_specs=[pl.BlockSpec((tm, d), lambda i: (i, 0)),
                   pl.BlockSpec((tm, LANES), lambda i: (i, 0))],
        out_shape=[jax.ShapeDtypeStruct((m, d), BF16),
                   jax.ShapeDtypeStruct((m, LANES), F32)],
        compiler_params=_cparams("parallel"),
        name="prologue",
    )(x2, g)


SECTION_ORDER = ("ag", "rg", "aq", "ak", "rq", "rk", "av", "rv")
SECTION_KIND = {"ag": "gate", "rg": "gate", "aq": "norm_rope", "ak": "norm_rope",
                "rq": "rope", "rk": "rope", "av": "plain", "rv": "plain"}
GAINED = ("aq", "ak", "rq", "rk")
ROW_CHUNK = 512


def _rope(y, cos, sin_up, sin_dn):
    q = HEAD_DIM // 4
    return y * cos + pltpu.roll(y, HEAD_DIM - q, 1) * sin_up + pltpu.roll(y, q, 1) * sin_dn


def _silu(y):
    h = 0.5 * y
    return h + h * jnp.tanh(h)


def _inproj_kernel(kind_ranges, n_tiles, n_steps, d_model, xg_ref, w_ref, ssq_ref, cos_ref, sup_ref, sdn_ref,
                   gain_ref, o_ref, acc_ref):
    t = pl.program_id(0)
    prev_col = lax.rem(t + (n_tiles - 1), n_tiles)
    tm, tn = acc_ref.shape

    def matmul():
        acc_ref[...] = jnp.dot(xg_ref[...], w_ref[...].astype(BF16), preferred_element_type=F32)

    def postprocess(kind):
        for c in range(tm // ROW_CHUNK):
            rows = slice(c * ROW_CHUNK, (c + 1) * ROW_CHUNK)
            r = lax.rsqrt(ssq_ref[rows, :] * (1.0 / d_model) + EPS)
            for g in range(tn // LANES):
                cols = slice(g * LANES, (g + 1) * LANES)
                y = acc_ref[rows, cols] * r
                if kind == "norm_rope":
                    ms = jnp.mean(y * y, axis=-1, keepdims=True)
                    y = _rope(y * lax.rsqrt(ms + EPS) * gain_ref[...],
                              cos_ref[rows, :], sup_ref[rows, :], sdn_ref[rows, :])
                elif kind == "rope":
                    y = _rope(y * gain_ref[...], cos_ref[rows, :], sup_ref[rows, :], sdn_ref[rows, :])
                elif kind == "gate":
                    y = _silu(y)
                o_ref[rows, cols] = y.astype(BF16)

    @pl.when(t == 0)
    def _():
        matmul()

    middle = jnp.logical_and(t > 0, t < n_steps - 1)
    for kind, lo, hi in kind_ranges:
        @pl.when(jnp.logical_and(middle, jnp.logical_and(prev_col >= lo, prev_col < hi)))
        def _(kind=kind):
            postprocess(kind)
            matmul()

    @pl.when(t == n_steps - 1)
    def _():
        postprocess(kind_ranges[-1][0])


def _inproj(xg, w, layer, ssq, tables, gains, *, seq, kind_ranges, gain_starts, tile_shifts, tn):
    m, d = xg.shape
    n = w.shape[2]
    n_tiles = n // tn
    tm = min(1024, seq)
    sb = seq // tm
    n_steps = (m // tm) * n_tiles + 1
    assert kind_ranges[-1][2] == n_tiles and tm % ROW_CHUNK == 0

    def cur(t):
        t = jnp.minimum(t, n_steps - 2)
        return t // n_tiles, lax.rem(t, n_tiles)

    def prev(t):
        t = jnp.maximum(t - 1, 0)
        return t // n_tiles, lax.rem(t, n_tiles)

    def gain_index(t):
        col = prev(t)[1]
        return (sum((col >= s).astype(jnp.int32) for s in gain_starts[1:]), 0, 0)

    def weight_index(t):
        col = cur(t)[1]
        shift = sum(jnp.where(jnp.logical_and(col >= lo, col < hi), sh, 0) for lo, hi, sh in tile_shifts)
        return (layer, 0, col + shift)

    kern = functools.partial(_inproj_kernel, kind_ranges, n_tiles, n_steps, d)
    tab = pl.BlockSpec((tm, LANES), lambda t: (lax.rem(prev(t)[0], sb), 0))
    return pl.pallas_call(
        kern,
        grid=(n_steps,),
        in_specs=[pl.BlockSpec((tm, d), lambda t: (cur(t)[0], 0)),
                  pl.BlockSpec((None, d, tn), weight_index),
                  pl.BlockSpec((tm, LANES), lambda t: (prev(t)[0], 0)),
                  tab, tab, tab,
                  pl.BlockSpec((None, 1, LANES), gain_index)],
        out_specs=pl.BlockSpec((tm, tn), lambda t: prev(t)),
        out_shape=jax.ShapeDtypeStruct((m, n), BF16),
        scratch_shapes=[pltpu.VMEM((tm, tn), F32)],
        compiler_params=_cparams("arbitrary"),
        name="inproj",
    )(xg, w, ssq, *tables, gains)


def _transpose_via_mxu(a):
    rows = lax.broadcasted_iota(jnp.int32, (LANES, LANES), 0)
    cols = lax.broadcasted_iota(jnp.int32, (LANES, LANES), 1)
    eye = jnp.where(rows == cols, 1.0, 0.0).astype(BF16)
    return lax.dot_general(eye, a, NT_DIMS, preferred_element_type=F32).astype(BF16)


SCORE_LIMIT = 50.0
Q_UNROLL = 2


def _max_sq_norm(a):
    af = a.astype(F32)
    ones = jnp.ones((8, LANES), BF16)
    norms = lax.dot_general(ones, (af * af).astype(BF16), NT_DIMS, preferred_element_type=F32)
    return jnp.max(norms)


def _attn_kernel(tq, tk, q_ref, k_ref, v_ref, g_ref, o_ref, vt_ref):
    seq = q_ref.shape[0]
    nq = ATTN_GROUP * tq
    vt_ref[...] = _transpose_via_mxu(v_ref[...])
    qmax2 = _max_sq_norm(q_ref[:, :LANES])
    for g in range(1, ATTN_GROUP):
        qmax2 = jnp.maximum(qmax2, _max_sq_norm(q_ref[:, g * LANES:(g + 1) * LANES]))
    bound2 = qmax2 * _max_sq_norm(k_ref[...]) * 1.02

    def load_q(qi):
        rows = pl.ds(pl.multiple_of(qi * tq, tq), tq)
        qb = q_ref[rows, :]
        qs = jnp.concatenate([qb[:, g * LANES:(g + 1) * LANES] for g in range(ATTN_GROUP)], axis=0)
        return rows, qs

    def finish(rows, o):
        for g in range(ATTN_GROUP):
            cols = slice(g * LANES, (g + 1) * LANES)
            og = o[:, g * tq:(g + 1) * tq].T
            o_ref[rows, cols] = (g_ref[rows, cols].astype(F32) * og).astype(BF16)

    def unshifted():
        def qblock(qi, carry):
            rows, qs = load_q(qi)

            def kvstep(c, st):
                l, acc = st
                c0 = pl.multiple_of(c * tk, tk)
                s = lax.dot_general(k_ref[pl.ds(c0, tk), :], qs, NT_DIMS,
                                    preferred_element_type=F32)
                p = jnp.exp2(s)
                l = l + jnp.sum(p.reshape(tk // 8, 8, nq), axis=0)
                return l, acc + jnp.dot(vt_ref[:, pl.ds(c0, tk)], p.astype(BF16), preferred_element_type=F32)

            init = (jnp.zeros((8, nq), F32), jnp.zeros((HEAD_DIM, nq), F32))
            l, acc = lax.fori_loop(0, seq // tk, kvstep, init, unroll=True)
            finish(rows, acc * (1.0 / jnp.sum(l, axis=0, keepdims=True)))
            return carry

        lax.fori_loop(0, seq // tq, qblock, 0, unroll=Q_UNROLL)

    def running_max():
        def qblock(qi, carry):
            rows, qs = load_q(qi)

            def kvstep(c, st):
                m, l, acc = st
                c0 = pl.multiple_of(c * tk, tk)
                s = lax.dot_general(k_ref[pl.ds(c0, tk), :], qs, NT_DIMS,
                                    preferred_element_type=F32)
                m_new = jnp.maximum(m, jnp.max(s, axis=0, keepdims=True))
                alpha = jnp.exp2(m - m_new)
                p = jnp.exp2(s - m_new)
                l = alpha * l + jnp.sum(p, axis=0, keepdims=True)
                pv = jnp.dot(vt_ref[:, pl.ds(c0, tk)], p.astype(BF16), preferred_element_type=F32)
                return m_new, l, alpha * acc + pv

            init = (jnp.full((1, nq), -1e30, F32), jnp.zeros((1, nq), F32), jnp.zeros((HEAD_DIM, nq), F32))
            _, l, acc = lax.fori_loop(0, seq // tk, kvstep, init)
            finish(rows, acc * (1.0 / l))
            return carry

        lax.fori_loop(0, seq // tq, qblock, 0)

    lax.cond(bound2 <= SCORE_LIMIT * SCORE_LIMIT, unshifted, running_max)


def _attention(proj, *, batch, seq, attn_width, kv_width, offs):
    kvh = kv_width // HEAD_DIM
    gw = ATTN_GROUP * HEAD_DIM
    tq = min(256, seq)
    tk = min(512, seq)
    assert offs["aq"] % gw == 0 and offs["ag"] % gw == 0
    q_off = offs["aq"] // gw
    k_off = offs["ak"] // HEAD_DIM
    v_off = offs["av"] // HEAD_DIM
    g_off = offs["ag"] // gw
    return pl.pallas_call(
        functools.partial(_attn_kernel, tq, tk),
        grid=(batch, kvh),
        in_specs=[pl.BlockSpec((seq, gw), lambda b, h: (b, q_off + h)),
                  pl.BlockSpec((seq, HEAD_DIM), lambda b, h: (b, k_off + h)),
                  pl.BlockSpec((seq, HEAD_DIM), lambda b, h: (b, v_off + h)),
                  pl.BlockSpec((seq, gw), lambda b, h: (b, g_off + h))],
        out_specs=pl.BlockSpec((seq, gw), lambda b, h: (b, h)),
        out_shape=jax.ShapeDtypeStruct((batch * seq, attn_width), BF16),
        scratch_shapes=[pltpu.VMEM((HEAD_DIM, seq), BF16)],
        compiler_params=_cparams("parallel", "parallel"),
        name="attention",
    )(proj, proj, proj, proj)


RET_UNROLL = 16


def _log_sigmoid(x):
    return jnp.minimum(x, 0.0) - jnp.log1p(jnp.exp(-jnp.abs(x)))


def _retention_kernel(ck, q_ref, k_ref, v_ref, g_ref, df_ref, db_ref, gn_ref, o_ref, kt_ref, sb_ref):
    seq = q_ref.shape[0]
    nchunk = seq // ck
    lgf = _log_sigmoid(df_ref[...])[:, :1]
    lgb = _log_sigmoid(db_ref[...])[:, :1]
    kt_ref[...] = _transpose_via_mxu(k_ref[...])

    diff = (lax.broadcasted_iota(jnp.int32, (ck, ck), 0)
            - lax.broadcasted_iota(jnp.int32, (ck, ck), 1)).astype(F32)
    decay = jnp.exp(jnp.where(diff >= 0, lgf * diff, -lgb * diff))
    rowi = lax.broadcasted_iota(jnp.int32, (ck, RET_QK_DIM), 0).astype(F32)
    xi_f = jnp.exp(lgf * (rowi + 1.0))
    xi_b = jnp.exp(lgb * (ck - rowi))
    coli = lax.broadcasted_iota(jnp.int32, (RET_QK_DIM, ck), 1).astype(F32)
    zt_f = jnp.exp(lgf * (ck - 1.0 - coli))
    zt_b = jnp.exp(lgb * coli)
    gc_f = jnp.exp(lgf * ck)
    gc_b = jnp.exp(lgb * ck)

    def kz(c0, zt):
        return (kt_ref[:, pl.ds(c0, ck)].astype(F32) * zt).astype(BF16)

    def bstep(t, state):
        jc = nchunk - 1 - t
        c0 = pl.multiple_of(jc * ck, ck)
        sb_ref[jc] = state.astype(BF16)
        upd = jnp.dot(kz(c0, zt_b), v_ref[pl.ds(c0, ck), :], preferred_element_type=F32)
        return gc_b * state + upd

    lax.fori_loop(0, nchunk, bstep, jnp.zeros((RET_QK_DIM, RET_V_DIM), F32), unroll=RET_UNROLL)

    def fstep(jc, state):
        c0 = pl.multiple_of(jc * ck, ck)
        rows = pl.ds(c0, ck)
        q = q_ref[rows, :]
        v = v_ref[rows, :]
        s = jnp.dot(q, kt_ref[:, pl.ds(c0, ck)], preferred_element_type=F32)
        out = jnp.dot((s * decay).astype(BF16), v, preferred_element_type=F32)
        qf = q.astype(F32)
        qx = jnp.concatenate([(qf * xi_f).astype(BF16), (qf * xi_b).astype(BF16)], axis=1)
        both = jnp.concatenate([state.astype(BF16), sb_ref[jc]], axis=0)
        out = out + jnp.dot(qx, both, preferred_element_type=F32)
        ms = jnp.mean(out * out, axis=-1, keepdims=True)
        y = out * lax.rsqrt(ms + EPS) * gn_ref[...]
        o_ref[rows, :] = (g_ref[rows, :].astype(F32) * y).astype(BF16)
        upd = jnp.dot(kz(c0, zt_f), v, preferred_element_type=F32)
        return gc_f * state + upd

    lax.fori_loop(0, nchunk, fstep, jnp.zeros((RET_QK_DIM, RET_V_DIM), F32), unroll=RET_UNROLL)


def _retention(proj, dfw, dbw, gnorm, *, batch, seq, offs, ret_heads):
    off_q, off_k, off_v, off_g = offs["rq"], offs["rk"], offs["rv"], offs["rg"]
    assert off_v % RET_V_DIM == 0 and off_g % RET_V_DIM == 0
    ck = min(256, seq)
    vec = lambda w: pl.BlockSpec((None, 1, w), lambda b, h: (h, 0, 0))
    return pl.pallas_call(
        functools.partial(_retention_kernel, ck),
        grid=(batch, ret_heads),
        in_specs=[pl.BlockSpec((seq, RET_QK_DIM), lambda b, h: (b, off_q // RET_QK_DIM + h)),
                  pl.BlockSpec((seq, RET_QK_DIM), lambda b, h: (b, off_k // RET_QK_DIM + h)),
                  pl.BlockSpec((seq, RET_V_DIM), lambda b, h: (b, off_v // RET_V_DIM + h)),
                  pl.BlockSpec((seq, RET_V_DIM), lambda b, h: (b, off_g // RET_V_DIM + h)),
                  vec(LANES), vec(LANES), vec(RET_V_DIM)],
        out_specs=pl.BlockSpec((seq, RET_V_DIM), lambda b, h: (b, h)),
        out_shape=jax.ShapeDtypeStruct((batch * seq, ret_heads * RET_V_DIM), BF16),
        scratch_shapes=[pltpu.VMEM((RET_QK_DIM, seq), BF16),
                        pltpu.VMEM((seq // ck, RET_QK_DIM, RET_V_DIM), BF16)],
        compiler_params=_cparams("parallel", "parallel"),
        name="retention",
    )(proj, proj, proj, proj, dfw, dbw, gnorm)


def _outproj_kernel(emit_next, n_tiles, n_steps, a_ref, r_ref, wa_ref, wr_ref, x_ref, *rest):
    if emit_next:
        g_ref, xo_ref, xg_ref, ssq_ref, acc_ref = rest
    else:
        xo_ref, acc_ref = rest
    t = pl.program_id(0)
    prev_col = lax.rem(t + (n_tiles - 1), n_tiles)

    def matmul():
        acc = jnp.dot(a_ref[...], wa_ref[...].astype(BF16), preferred_element_type=F32)
        acc_ref[...] = acc + jnp.dot(r_ref[...], wr_ref[...].astype(BF16), preferred_element_type=F32)

    def finish():
        xn = x_ref[...] + acc_ref[...]
        xo_ref[...] = xn
        if emit_next:
            xg_ref[...] = (xn * g_ref[...]).astype(BF16)
            return jnp.broadcast_to(jnp.sum(xn * xn, axis=-1, keepdims=True), ssq_ref.shape)
        return None

    def accumulate(part):
        if emit_next:
            @pl.when(prev_col == 0)
            def _():
                ssq_ref[...] = part

            @pl.when(prev_col > 0)
            def _():
                ssq_ref[...] = ssq_ref[...] + part

    @pl.when(t == 0)
    def _():
        matmul()

    @pl.when(jnp.logical_and(t > 0, t < n_steps - 1))
    def _():
        part = finish()
        matmul()
        accumulate(part)

    @pl.when(t == n_steps - 1)
    def _():
        accumulate(finish())


def _outproj(a, r, w, layer, x2, gnext):
    m, d = x2.shape
    ka, kr = a.shape[1], r.shape[1]
    assert ka == kr and w.shape[1] == ka + kr
    tm = min(1024, m)
    tn = min(512, d)
    n_tiles = d // tn
    n_steps = (m // tm) * n_tiles + 1
    emit_next = gnext is not None

    def cur(t):
        t = jnp.minimum(t, n_steps - 2)
        return t // n_tiles, lax.rem(t, n_tiles)

    def prev(t):
        t = jnp.maximum(t - 1, 0)
        return t // n_tiles, lax.rem(t, n_tiles)

    in_specs = [pl.BlockSpec((tm, ka), lambda t: (cur(t)[0], 0)),
                pl.BlockSpec((tm, kr), lambda t: (cur(t)[0], 0)),
                pl.BlockSpec((None, ka, tn), lambda t: (layer, 0, cur(t)[1])),
                pl.BlockSpec((None, kr, tn), lambda t: (layer, 1, cur(t)[1])),
                pl.BlockSpec((tm, tn), lambda t: prev(t))]
    out_specs = [pl.BlockSpec((tm, tn), lambda t: prev(t))]
    out_shape = [jax.ShapeDtypeStruct((m, d), F32)]
    args = [a, r, w, w, x2]
    if emit_next:
        in_specs.append(pl.BlockSpec((1, tn), lambda t: (0, prev(t)[1])))
        out_specs += [pl.BlockSpec((tm, tn), lambda t: prev(t)),
                      pl.BlockSpec((tm, LANES), lambda t: (prev(t)[0], 0))]
        out_shape += [jax.ShapeDtypeStruct((m, d), BF16),
                      jax.ShapeDtypeStruct((m, LANES), F32)]
        args.append(gnext)
    return pl.pallas_call(
        functools.partial(_outproj_kernel, emit_next, n_tiles, n_steps),
        grid=(n_steps,),
        in_specs=in_specs,
        out_specs=out_specs,
        out_shape=out_shape,
        scratch_shapes=[pltpu.VMEM((tm, tn), F32)],
        compiler_params=_cparams("arbitrary"),
        name="outproj",
    )(*args)


def _final_kernel(x_ref, g_ref, o_ref):
    x = x_ref[...]
    ms = jnp.mean(x * x, axis=-1, keepdims=True)
    o_ref[...] = x * lax.rsqrt(ms + EPS) * g_ref[...]


def _final_norm(x2, g):
    m, d = x2.shape
    tm = min(256, m)
    return pl.pallas_call(
        _final_kernel,
        grid=(m // tm,),
        in_specs=[pl.BlockSpec((tm, d), lambda i: (i, 0)),
                  pl.BlockSpec((1, d), lambda i: (0, 0))],
        out_specs=pl.BlockSpec((tm, d), lambda i: (i, 0)),
        out_shape=jax.ShapeDtypeStruct((m, d), F32),
        compiler_params=_cparams("parallel"),
        name="final_norm",
    )(x2, g)


def _rope_tables(seq):
    rows = seq // GRID_W
    row = jnp.repeat(jnp.arange(rows), GRID_W).astype(F32)
    col = jnp.tile(jnp.arange(GRID_W), rows).astype(F32)
    axis_dim = HEAD_DIM // 2
    inv = ROPE_THETA ** (-jnp.arange(0, axis_dim, 2, dtype=F32) / axis_dim)
    ang_r = row[:, None] * inv[None, :]
    ang_c = col[:, None] * inv[None, :]
    cr, sr, cc, sc = jnp.cos(ang_r), jnp.sin(ang_r), jnp.cos(ang_c), jnp.sin(ang_c)
    z = jnp.zeros_like(sr)
    cos = jnp.concatenate([cr, cr, cc, cc], axis=-1)
    sin_up = jnp.concatenate([-sr, z, -sc, z], axis=-1)
    sin_dn = jnp.concatenate([z, sr, z, sc], axis=-1)
    return cos, sin_up, sin_dn


def kernel(x, norm_w, w_in, q_norm, k_norm, ret_decay_fwd, ret_decay_bwd, ret_norm, w_out, final_norm):
    batch, seq, d = x.shape
    depth = w_in.shape[0]
    attn_width = d // 2
    kv_width = attn_width // ATTN_GROUP
    ret_width = d - attn_width
    ret_heads = ret_width // RET_V_DIM
    ret_qk_width = ret_heads * RET_QK_DIM
    m = batch * seq

    ref_widths = (("aq", attn_width), ("ak", kv_width), ("av", kv_width), ("ag", attn_width),
                  ("rq", ret_qk_width), ("rk", ret_qk_width), ("rv", ret_width), ("rg", ret_width))
    ref_pos, pos = {}, 0
    for name, wdt in ref_widths:
        ref_pos[name] = (pos, wdt)
        pos += wdt
    assert pos == w_in.shape[2]
    tn = min(512, kv_width)
    offs, kind_ranges, gain_starts, pos = {}, [], [], 0
    for name in SECTION_ORDER:
        wdt = ref_pos[name][1]
        assert wdt % tn == 0
        offs[name] = pos
        kind = SECTION_KIND[name]
        if kind_ranges and kind_ranges[-1][0] == kind:
            kind_ranges[-1] = (kind, kind_ranges[-1][1], (pos + wdt) // tn)
        else:
            kind_ranges.append((kind, pos // tn, (pos + wdt) // tn))
        if name in GAINED:
            gain_starts.append(pos // tn)
        pos += wdt

    tile_shifts = tuple((offs[name] // tn, (offs[name] + ref_pos[name][1]) // tn, (ref_pos[name][0] - offs[name]) // tn)
                        for name in SECTION_ORDER)
    w_in_b, w_out_b = w_in, w_out
    tables = _rope_tables(seq)
    attn_scale = HEAD_DIM ** -0.5
    x2 = x.reshape(m, d)

    xg, ssq = _prologue(x2, norm_w[0].reshape(1, d))
    for l in range(depth):
        gains = jnp.stack([q_norm[l] * (attn_scale * LOG2E),
                           k_norm[l],
                           jnp.ones((HEAD_DIM,), F32),
                           jnp.full((HEAD_DIM,), RET_QK_DIM ** -0.5, F32)]).reshape(len(GAINED), 1, HEAD_DIM)
        proj = _inproj(xg, w_in_b, l, ssq, tables, gains, seq=seq, kind_ranges=tuple(kind_ranges),
                       gain_starts=tuple(gain_starts), tile_shifts=tile_shifts, tn=tn)
        a = _attention(proj, batch=batch, seq=seq, attn_width=attn_width, kv_width=kv_width, offs=offs)
        dfw = jnp.broadcast_to(ret_decay_fwd[l][:, None, None], (ret_heads, 1, LANES))
        dbw = jnp.broadcast_to(ret_decay_bwd[l][:, None, None], (ret_heads, 1, LANES))
        r = _retention(proj, dfw, dbw, ret_norm[l].reshape(ret_heads, 1, RET_V_DIM),
                       batch=batch, seq=seq, offs=offs, ret_heads=ret_heads)
        if l + 1 < depth:
            x2, xg, ssq = _outproj(a, r, w_out_b, l, x2, norm_w[l + 1].reshape(1, d))
        else:
            (x2,) = _outproj(a, r, w_out_b, l, x2, None)
    out = _final_norm(x2, final_norm.reshape(1, d))
    return out.reshape(batch, seq, d)
```

```python
import functools
import math

import jax
import jax.numpy as jnp
from jax import lax
from jax.experimental import pallas as pl
from jax.experimental.pallas import tpu as pltpu

F32 = jnp.float32
BF16 = jnp.bfloat16

GRID_W = 64
HEAD_DIM = 128
ATTN_GROUP = 4
RET_V_DIM = 256
RET_QK_DIM = 128
ROPE_THETA = 10000.0
EPS = 1e-6
LOG2E = math.log2(math.e)

LANES = 128
VMEM_LIMIT = 56 * 1024 * 1024

NT_DIMS = (((1,), (1,)), ((), ()))


def _cparams(*sem):
    return pltpu.CompilerParams(dimension_semantics=sem, vmem_limit_bytes=VMEM_LIMIT)


def _prologue_kernel(x_ref, g_ref, xg_ref, ssq_ref):
    x = x_ref[...]
    xg_ref[...] = (x * g_ref[...]).astype(BF16)
    ssq_ref[...] = jnp.broadcast_to(jnp.sum(x * x, axis=-1, keepdims=True), ssq_ref.shape)


def _prologue(x2, g):
    m, d = x2.shape
    tm = min(256, m)
    return pl.pallas_call(
        _prologue_kernel,
        grid=(m // tm,),
        in_specs=[pl.BlockSpec((tm, d), lambda i: (i, 0)),
                  pl.BlockSpec((1, d), lambda i: (0, 0))],
        out_specs=[pl.BlockSpec((tm, d), lambda i: (i, 0)),
                   pl.BlockSpec((tm, LANES), lambda i: (i, 0))],
        out_shape=[jax.ShapeDtypeStruct((m, d), BF16),
                   jax.ShapeDtypeStruct((m, LANES), F32)],
        compiler_params=_cparams("parallel"),
        name="prologue",
    )(x2, g)


SECTION_ORDER = ("ag", "rg", "aq", "ak", "rq", "rk", "av", "rv")
SECTION_KIND = {"ag": "gate", "rg": "gate", "aq": "norm_rope", "ak": "norm_rope",
                "rq": "rope", "rk": "rope", "av": "plain", "rv": "plain"}
GAINED = ("aq", "ak", "rq", "rk")
ROW_CHUNK = 512


def _rope(y, cos, sin_up, sin_dn):
    q = HEAD_DIM // 4
    return y * cos + pltpu.roll(y, HEAD_DIM - q, 1) * sin_up + pltpu.roll(y, q, 1) * sin_dn


def _silu(y):
    h = 0.5 * y
    return h + h * jnp.tanh(h)


def _inproj_kernel(kind_ranges, n_tiles, n_steps, d_model, xg_ref, w_ref, ssq_ref, cos_ref, sup_ref, sdn_ref,
                   gain_ref, o_ref, acc_ref):
    t = pl.program_id(0)
    prev_col = lax.rem(t + (n_tiles - 1), n_tiles)
    tm, tn = acc_ref.shape

    def matmul():
        acc_ref[...] = jnp.dot(xg_ref[...], w_ref[...].astype(BF16), preferred_element_type=F32)

    def postprocess(kind):
        for c in range(tm // ROW_CHUNK):
            rows = slice(c * ROW_CHUNK, (c + 1) * ROW_CHUNK)
            r = lax.rsqrt(ssq_ref[rows, :] * (1.0 / d_model) + EPS)
            for g in range(tn // LANES):
                cols = slice(g * LANES, (g + 1) * LANES)
                y = acc_ref[rows, cols] * r
                if kind == "norm_rope":
                    ms = jnp.mean(y * y, axis=-1, keepdims=True)
                    y = _rope(y * lax.rsqrt(ms + EPS) * gain_ref[...],
                              cos_ref[rows, :], sup_ref[rows, :], sdn_ref[rows, :])
                elif kind == "rope":
                    y = _rope(y * gain_ref[...], cos_ref[rows, :], sup_ref[rows, :], sdn_ref[rows, :])
                elif kind == "gate":
                    y = _silu(y)
                o_ref[rows, cols] = y.astype(BF16)

    @pl.when(t == 0)
    def _():
        matmul()

    middle = jnp.logical_and(t > 0, t < n_steps - 1)
    for kind, lo, hi in kind_ranges:
        @pl.when(jnp.logical_and(middle, jnp.logical_and(prev_col >= lo, prev_col < hi)))
        def _(kind=kind):
            postprocess(kind)
            matmul()

    @pl.when(t == n_steps - 1)
    def _():
        postprocess(kind_ranges[-1][0])


def _inproj(xg, w, layer, ssq, tables, gains, *, seq, kind_ranges, gain_starts, tile_shifts, tn):
    m, d = xg.shape
    n = w.shape[2]
    n_tiles = n // tn
    tm = min(1024, seq)
    sb = seq // tm
    n_steps = (m // tm) * n_tiles + 1
    assert kind_ranges[-1][2] == n_tiles and tm % ROW_CHUNK == 0

    def cur(t):
        t = jnp.minimum(t, n_steps - 2)
        return t // n_tiles, lax.rem(t, n_tiles)

    def prev(t):
        t = jnp.maximum(t - 1, 0)
        return t // n_tiles, lax.rem(t, n_tiles)

    def gain_index(t):
        col = prev(t)[1]
        return (sum((col >= s).astype(jnp.int32) for s in gain_starts[1:]), 0, 0)

    def weight_index(t):
        col = cur(t)[1]
        shift = sum(jnp.where(jnp.logical_and(col >= lo, col < hi), sh, 0) for lo, hi, sh in tile_shifts)
        return (layer, 0, col + shift)

    kern = functools.partial(_inproj_kernel, kind_ranges, n_tiles, n_steps, d)
    tab = pl.BlockSpec((tm, LANES), lambda t: (lax.rem(prev(t)[0], sb), 0))
    return pl.pallas_call(
        kern,
        grid=(n_steps,),
        in_specs=[pl.BlockSpec((tm, d), lambda t: (cur(t)[0], 0)),
                  pl.BlockSpec((None, d, tn), weight_index),
                  pl.BlockSpec((tm, LANES), lambda t: (prev(t)[0], 0)),
                  tab, tab, tab,
                  pl.BlockSpec((None, 1, LANES), gain_index)],
        out_specs=pl.BlockSpec((tm, tn), lambda t: prev(t)),
        out_shape=jax.ShapeDtypeStruct((m, n), BF16),
        scratch_shapes=[pltpu.VMEM((tm, tn), F32)],
        compiler_params=_cparams("arbitrary"),
        name="inproj",
    )(xg, w, ssq, *tables, gains)


def _transpose_via_mxu(a):
    rows = lax.broadcasted_iota(jnp.int32, (LANES, LANES), 0)
    cols = lax.broadcasted_iota(jnp.int32, (LANES, LANES), 1)
    eye = jnp.where(rows == cols, 1.0, 0.0).astype(BF16)
    return lax.dot_general(eye, a, NT_DIMS, preferred_element_type=F32).astype(BF16)


SCORE_LIMIT = 50.0
Q_UNROLL = 2


def _attn_kernel(tq, tk, q_ref, k_ref, v_ref, g_ref, gains_ref, o_ref, vt_ref):
    seq = q_ref.shape[0]
    nq = ATTN_GROUP * tq
    vt_ref[...] = _transpose_via_mxu(v_ref[...])
    gain_q = jnp.max(jnp.abs(gains_ref[0:1, :]))
    gain_k = jnp.max(jnp.abs(gains_ref[1:2, :]))
    bound = gain_q * gain_k * (HEAD_DIM * 1.02)

    def load_q(qi):
        rows = pl.ds(pl.multiple_of(qi * tq, tq), tq)
        qb = q_ref[rows, :]
        qs = jnp.concatenate([qb[:, g * LANES:(g + 1) * LANES] for g in range(ATTN_GROUP)], axis=0)
        return rows, qs

    def finish(rows, o):
        for g in range(ATTN_GROUP):
            cols = slice(g * LANES, (g + 1) * LANES)
            og = o[:, g * tq:(g + 1) * tq].T
            o_ref[rows, cols] = (g_ref[rows, cols].astype(F32) * og).astype(BF16)

    def unshifted():
        def qblock(qi, carry):
            rows, qs = load_q(qi)

            def kvstep(c, st):
                l, acc = st
                c0 = pl.multiple_of(c * tk, tk)
                s = lax.dot_general(k_ref[pl.ds(c0, tk), :], qs, NT_DIMS,
                                    preferred_element_type=F32)
                p = jnp.exp2(s)
                l = l + jnp.sum(p.reshape(tk // 8, 8, nq), axis=0)
                return l, acc + jnp.dot(vt_ref[:, pl.ds(c0, tk)], p.astype(BF16), preferred_element_type=F32)

            init = (jnp.zeros((8, nq), F32), jnp.zeros((HEAD_DIM, nq), F32))
            l, acc = lax.fori_loop(0, seq // tk, kvstep, init, unroll=True)
            finish(rows, acc * (1.0 / jnp.sum(l, axis=0, keepdims=True)))
            return carry

        lax.fori_loop(0, seq // tq, qblock, 0, unroll=Q_UNROLL)

    def running_max():
        def qblock(qi, carry):
            rows, qs = load_q(qi)

            def kvstep(c, st):
                m, l, acc = st
                c0 = pl.multiple_of(c * tk, tk)
                s = lax.dot_general(k_ref[pl.ds(c0, tk), :], qs, NT_DIMS,
                                    preferred_element_type=F32)
                m_new = jnp.maximum(m, jnp.max(s, axis=0, keepdims=True))
                alpha = jnp.exp2(m - m_new)
                p = jnp.exp2(s - m_new)
                l = alpha * l + jnp.sum(p, axis=0, keepdims=True)
                pv = jnp.dot(vt_ref[:, pl.ds(c0, tk)], p.astype(BF16), preferred_element_type=F32)
                return m_new, l, alpha * acc + pv

            init = (jnp.full((1, nq), -1e30, F32), jnp.zeros((1, nq), F32), jnp.zeros((HEAD_DIM, nq), F32))
            _, l, acc = lax.fori_loop(0, seq // tk, kvstep, init)
            finish(rows, acc * (1.0 / l))
            return carry

        lax.fori_loop(0, seq // tq, qblock, 0)

    lax.cond(bound <= SCORE_LIMIT, unshifted, running_max)


def _attention(proj, gains, *, batch, seq, attn_width, kv_width, offs):
    kvh = kv_width // HEAD_DIM
    gw = ATTN_GROUP * HEAD_DIM
    tq = min(256, seq)
    tk = min(1024, seq)
    assert offs["aq"] % gw == 0 and offs["ag"] % gw == 0
    q_off = offs["aq"] // gw
    k_off = offs["ak"] // HEAD_DIM
    v_off = offs["av"] // HEAD_DIM
    g_off = offs["ag"] // gw
    return pl.pallas_call(
        functools.partial(_attn_kernel, tq, tk),
        grid=(batch, kvh),
        in_specs=[pl.BlockSpec((seq, gw), lambda b, h: (b, q_off + h)),
                  pl.BlockSpec((seq, HEAD_DIM), lambda b, h: (b, k_off + h)),
                  pl.BlockSpec((seq, HEAD_DIM), lambda b, h: (b, v_off + h)),
                  pl.BlockSpec((seq, gw), lambda b, h: (b, g_off + h)),
                  pl.BlockSpec(gains.shape, lambda b, h: (0, 0))],
        out_specs=pl.BlockSpec((seq, gw), lambda b, h: (b, h)),
        out_shape=jax.ShapeDtypeStruct((batch * seq, attn_width), BF16),
        scratch_shapes=[pltpu.VMEM((HEAD_DIM, seq), BF16)],
        compiler_params=_cparams("parallel", "parallel"),
        name="attention",
    )(proj, proj, proj, proj, gains)


RET_UNROLL = 16


def _log_sigmoid(x):
    return jnp.minimum(x, 0.0) - jnp.log1p(jnp.exp(-jnp.abs(x)))


def _retention_kernel(ck, q_ref, k_ref, v_ref, g_ref, df_ref, db_ref, gn_ref, o_ref, kt_ref, sb_ref):
    seq = q_ref.shape[0]
    nchunk = seq // ck
    lgf = _log_sigmoid(df_ref[...])[:, :1]
    lgb = _log_sigmoid(db_ref[...])[:, :1]
    kt_ref[...] = _transpose_via_mxu(k_ref[...])

    diff = (lax.broadcasted_iota(jnp.int32, (ck, ck), 0)
            - lax.broadcasted_iota(jnp.int32, (ck, ck), 1)).astype(F32)
    decay = jnp.exp(jnp.where(diff >= 0, lgf * diff, -lgb * diff))
    rowi = lax.broadcasted_iota(jnp.int32, (ck, RET_QK_DIM), 0).astype(F32)
    xi_f = jnp.exp(lgf * (rowi + 1.0))
    xi_b = jnp.exp(lgb * (ck - rowi))
    coli = lax.broadcasted_iota(jnp.int32, (RET_QK_DIM, ck), 1).astype(F32)
    zt_f = jnp.exp(lgf * (ck - 1.0 - coli))
    zt_b = jnp.exp(lgb * coli)
    gc_f = jnp.exp(lgf * ck)
    gc_b = jnp.exp(lgb * ck)

    def kz(c0, zt):
        return (kt_ref[:, pl.ds(c0, ck)].astype(F32) * zt).astype(BF16)

    def bstep(t, state):
        jc = nchunk - 1 - t
        c0 = pl.multiple_of(jc * ck, ck)
        sb_ref[jc] = state.astype(BF16)
        upd = jnp.dot(kz(c0, zt_b), v_ref[pl.ds(c0, ck), :], preferred_element_type=F32)
        return gc_b * state + upd

    lax.fori_loop(0, nchunk, bstep, jnp.zeros((RET_QK_DIM, RET_V_DIM), F32), unroll=RET_UNROLL)

    def fstep(jc, state):
        c0 = pl.multiple_of(jc * ck, ck)
        rows = pl.ds(c0, ck)
        q = q_ref[rows, :]
        v = v_ref[rows, :]
        s = jnp.dot(q, kt_ref[:, pl.ds(c0, ck)], preferred_element_type=F32)
        out = jnp.dot((s * decay).astype(BF16), v, preferred_element_type=F32)
        qf = q.astype(F32)
        qx = jnp.concatenate([(qf * xi_f).astype(BF16), (qf * xi_b).astype(BF16)], axis=1)
        both = jnp.concatenate([state.astype(BF16), sb_ref[jc]], axis=0)
        out = out + jnp.dot(qx, both, preferred_element_type=F32)
        ms = jnp.mean(out * out, axis=-1, keepdims=True)
        y = out * lax.rsqrt(ms + EPS) * gn_ref[...]
        o_ref[rows, :] = (g_ref[rows, :].astype(F32) * y).astype(BF16)
        upd = jnp.dot(kz(c0, zt_f), v, preferred_element_type=F32)
        return gc_f * state + upd

    lax.fori_loop(0, nchunk, fstep, jnp.zeros((RET_QK_DIM, RET_V_DIM), F32), unroll=RET_UNROLL)


def _retention(proj, dfw, dbw, gnorm, *, batch, seq, offs, ret_heads):
    off_q, off_k, off_v, off_g = offs["rq"], offs["rk"], offs["rv"], offs["rg"]
    assert off_v % RET_V_DIM == 0 and off_g % RET_V_DIM == 0
    ck = min(256, seq)
    vec = lambda w: pl.BlockSpec((None, 1, w), lambda b, h: (h, 0, 0))
    return pl.pallas_call(
        functools.partial(_retention_kernel, ck),
        grid=(batch, ret_heads),
        in_specs=[pl.BlockSpec((seq, RET_QK_DIM), lambda b, h: (b, off_q // RET_QK_DIM + h)),
                  pl.BlockSpec((seq, RET_QK_DIM), lambda b, h: (b, off_k // RET_QK_DIM + h)),
                  pl.BlockSpec((seq, RET_V_DIM), lambda b, h: (b, off_v // RET_V_DIM + h)),
                  pl.BlockSpec((seq, RET_V_DIM), lambda b, h: (b, off_g // RET_V_DIM + h)),
                  vec(LANES), vec(LANES), vec(RET_V_DIM)],
        out_specs=pl.BlockSpec((seq, RET_V_DIM), lambda b, h: (b, h)),
        out_shape=jax.ShapeDtypeStruct((batch * seq, ret_heads * RET_V_DIM), BF16),
        scratch_shapes=[pltpu.VMEM((RET_QK_DIM, seq), BF16),
                        pltpu.VMEM((seq // ck, RET_QK_DIM, RET_V_DIM), BF16)],
        compiler_params=_cparams("parallel", "parallel"),
        name="retention",
    )(proj, proj, proj, proj, dfw, dbw, gnorm)


def _outproj_kernel(emit_next, n_tiles, n_steps, a_ref, r_ref, wa_ref, wr_ref, x_ref, *rest):
    if emit_next:
        g_ref, xo_ref, xg_ref, ssq_ref, acc_ref = rest
    else:
        xo_ref, acc_ref = rest
    t = pl.program_id(0)
    prev_col = lax.rem(t + (n_tiles - 1), n_tiles)

    def matmul():
        acc = jnp.dot(a_ref[...], wa_ref[...], preferred_element_type=F32)
        acc_ref[...] = acc + jnp.dot(r_ref[...], wr_ref[...], preferred_element_type=F32)

    def finish():
        xn = x_ref[...] + acc_ref[...]
        xo_ref[...] = xn
        if emit_next:
            xg_ref[...] = (xn * g_ref[...]).astype(BF16)
            return jnp.broadcast_to(jnp.sum(xn * xn, axis=-1, keepdims=True), ssq_ref.shape)
        return None

    def accumulate(part):
        if emit_next:
            @pl.when(prev_col == 0)
            def _():
                ssq_ref[...] = part

            @pl.when(prev_col > 0)
            def _():
                ssq_ref[...] = ssq_ref[...] + part

    @pl.when(t == 0)
    def _():
        matmul()

    @pl.when(jnp.logical_and(t > 0, t < n_steps - 1))
    def _():
        part = finish()
        matmul()
        accumulate(part)

    @pl.when(t == n_steps - 1)
    def _():
        accumulate(finish())


def _outproj(a, r, w, layer, x2, gnext):
    m, d = x2.shape
    ka, kr = a.shape[1], r.shape[1]
    assert ka == kr and w.shape[1] == ka + kr
    tm = min(1024, m)
    tn = min(512, d)
    n_tiles = d // tn
    n_steps = (m // tm) * n_tiles + 1
    emit_next = gnext is not None

    def cur(t):
        t = jnp.minimum(t, n_steps - 2)
        return t // n_tiles, lax.rem(t, n_tiles)

    def prev(t):
        t = jnp.maximum(t - 1, 0)
        return t // n_tiles, lax.rem(t, n_tiles)

    in_specs = [pl.BlockSpec((tm, ka), lambda t: (cur(t)[0], 0)),
                pl.BlockSpec((tm, kr), lambda t: (cur(t)[0], 0)),
                pl.BlockSpec((None, ka, tn), lambda t: (layer, 0, cur(t)[1])),
                pl.BlockSpec((None, kr, tn), lambda t: (layer, 1, cur(t)[1])),
                pl.BlockSpec((tm, tn), lambda t: prev(t))]
    out_specs = [pl.BlockSpec((tm, tn), lambda t: prev(t))]
    out_shape = [jax.ShapeDtypeStruct((m, d), F32)]
    args = [a, r, w, w, x2]
    if emit_next:
        in_specs.append(pl.BlockSpec((1, tn), lambda t: (0, prev(t)[1])))
        out_specs += [pl.BlockSpec((tm, tn), lambda t: prev(t)),
                      pl.BlockSpec((tm, LANES), lambda t: (prev(t)[0], 0))]
        out_shape += [jax.ShapeDtypeStruct((m, d), BF16),
                      jax.ShapeDtypeStruct((m, LANES), F32)]
        args.append(gnext)
    return pl.pallas_call(
        functools.partial(_outproj_kernel, emit_next, n_tiles, n_steps),
        grid=(n_steps,),
        in_specs=in_specs,
        out_specs=out_specs,
        out_shape=out_shape,
        scratch_shapes=[pltpu.VMEM((tm, tn), F32)],
        compiler_params=_cparams("arbitrary"),
        name="outproj",
    )(*args)


def _final_kernel(x_ref, g_ref, o_ref):
    x = x_ref[...]
    ms = jnp.mean(x * x, axis=-1, keepdims=True)
    o_ref[...] = x * lax.rsqrt(ms + EPS) * g_ref[...]


def _final_norm(x2, g):
    m, d = x2.shape
    tm = min(256, m)
    return pl.pallas_call(
        _final_kernel,
        grid=(m // tm,),
        in_specs=[pl.BlockSpec((tm, d), lambda i: (i, 0)),
                  pl.BlockSpec((1, d), lambda i: (0, 0))],
        out_specs=pl.BlockSpec((tm, d), lambda i: (i, 0)),
        out_shape=jax.ShapeDtypeStruct((m, d), F32),
        compiler_params=_cparams("parallel"),
        name="final_norm",
    )(x2, g)


def _rope_tables(seq):
    rows = seq // GRID_W
    row = jnp.repeat(jnp.arange(rows), GRID_W).astype(F32)
    col = jnp.tile(jnp.arange(GRID_W), rows).astype(F32)
    axis_dim = HEAD_DIM // 2
    inv = ROPE_THETA ** (-jnp.arange(0, axis_dim, 2, dtype=F32) / axis_dim)
    ang_r = row[:, None] * inv[None, :]
    ang_c = col[:, None] * inv[None, :]
    cr, sr, cc, sc = jnp.cos(ang_r), jnp.sin(ang_r), jnp.cos(ang_c), jnp.sin(ang_c)
    z = jnp.zeros_like(sr)
    cos = jnp.concatenate([cr, cr, cc, cc], axis=-1)
    sin_up = jnp.concatenate([-sr, z, -sc, z], axis=-1)
    sin_dn = jnp.concatenate([z, sr, z, sc], axis=-1)
    return cos, sin_up, sin_dn


def kernel(x, norm_w, w_in, q_norm, k_norm, ret_decay_fwd, ret_decay_bwd, ret_norm, w_out, final_norm):
    batch, seq, d = x.shape
    depth = w_in.shape[0]
    attn_width = d // 2
    kv_width = attn_width // ATTN_GROUP
    ret_width = d - attn_width
    ret_heads = ret_width // RET_V_DIM
    ret_qk_width = ret_heads * RET_QK_DIM
    m = batch * seq

    ref_widths = (("aq", attn_width), ("ak", kv_width), ("av", kv_width), ("ag", attn_width),
                  ("rq", ret_qk_width), ("rk", ret_qk_width), ("rv", ret_width), ("rg", ret_width))
    ref_pos, pos = {}, 0
    for name, wdt in ref_widths:
        ref_pos[name] = (pos, wdt)
        pos += wdt
    assert pos == w_in.shape[2]
    tn = min(512, kv_width)
    offs, kind_ranges, gain_starts, pos = {}, [], [], 0
    for name in SECTION_ORDER:
        wdt = ref_pos[name][1]
        assert wdt % tn == 0
        offs[name] = pos
        kind = SECTION_KIND[name]
        if kind_ranges and kind_ranges[-1][0] == kind:
            kind_ranges[-1] = (kind, kind_ranges[-1][1], (pos + wdt) // tn)
        else:
            kind_ranges.append((kind, pos // tn, (pos + wdt) // tn))
        if name in GAINED:
            gain_starts.append(pos // tn)
        pos += wdt

    tile_shifts = tuple((offs[name] // tn, (offs[name] + ref_pos[name][1]) // tn, (ref_pos[name][0] - offs[name]) // tn)
                        for name in SECTION_ORDER)
    w_out_b = w_out.astype(BF16)
    tables = _rope_tables(seq)
    attn_scale = HEAD_DIM ** -0.5
    x2 = x.reshape(m, d)

    xg, ssq = _prologue(x2, norm_w[0].reshape(1, d))
    for l in range(depth):
        gains = jnp.stack([q_norm[l] * (attn_scale * LOG2E),
                           k_norm[l],
                           jnp.ones((HEAD_DIM,), F32),
                           jnp.full((HEAD_DIM,), RET_QK_DIM ** -0.5, F32)]).reshape(len(GAINED), 1, HEAD_DIM)
        proj = _inproj(xg, w_in, l, ssq, tables, gains, seq=seq, kind_ranges=tuple(kind_ranges),
                       gain_starts=tuple(gain_starts), tile_shifts=tile_shifts, tn=tn)
        a = _attention(proj, gains.reshape(len(GAINED), HEAD_DIM), batch=batch, seq=seq, attn_width=attn_width, kv_width=kv_width, offs=offs)
        dfw = jnp.broadcast_to(ret_decay_fwd[l][:, None, None], (ret_heads, 1, LANES))
        dbw = jnp.broadcast_to(ret_decay_bwd[l][:, None, None], (ret_heads, 1, LANES))
        r = _retention(proj, dfw, dbw, ret_norm[l].reshape(ret_heads, 1, RET_V_DIM),
                       batch=batch, seq=seq, offs=offs, ret_heads=ret_heads)
        if l + 1 < depth:
            x2, xg, ssq = _outproj(a, r, w_out_b, l, x2, norm_w[l + 1].reshape(1, d))
        else:
            (x2,) = _outproj(a, r, w_out_b, l, x2, None)
    out = _final_norm(x2, final_norm.reshape(1, d))
    return out.reshape(batch, seq, d)
```

```python
import functools
import math

import jax
import jax.numpy as jnp
from jax import lax
from jax.experimental import pallas as pl
from jax.experimental.pallas import tpu as pltpu

F32 = jnp.float32
BF16 = jnp.bfloat16

GRID_W = 64
HEAD_DIM = 128
ATTN_GROUP = 4
RET_V_DIM = 256
RET_QK_DIM = 128
ROPE_THETA = 10000.0
EPS = 1e-6
LOG2E = math.log2(math.e)

LANES = 128
VMEM_LIMIT = 56 * 1024 * 1024

NT_DIMS = (((1,), (1,)), ((), ()))


def _cparams(*sem):
    return pltpu.CompilerParams(dimension_semantics=sem, vmem_limit_bytes=VMEM_LIMIT)


def _prologue_kernel(x_ref, g_ref, xg_ref, ssq_ref):
    x = x_ref[...]
    xg_ref[...] = (x * g_ref[...]).astype(BF16)
    ssq_ref[...] = jnp.broadcast_to(jnp.sum(x * x, axis=-1, keepdims=True), ssq_ref.shape)


def _prologue(x2, g):
    m, d = x2.shape
    tm = min(256, m)
    return pl.pallas_call(
        _prologue_kernel,
        grid=(m // tm,),
        in_specs=[pl.BlockSpec((tm, d), lambda i: (i, 0)),
                  pl.BlockSpec((1, d), lambda i: (0, 0))],
        out_specs=[pl.BlockSpec((tm, d), lambda i: (i, 0)),
                   pl.BlockSpec((tm, LANES), lambda i: (i, 0))],
        out_shape=[jax.ShapeDtypeStruct((m, d), BF16),
                   jax.ShapeDtypeStruct((m, LANES), F32)],
        compiler_params=_cparams("parallel"),
        name="prologue",
    )(x2, g)


SECTION_ORDER = ("ag", "rg", "aq", "ak", "rq", "rk", "av", "rv")
SECTION_KIND = {"ag": "gate", "rg": "gate", "aq": "norm_rope", "ak": "norm_rope",
                "rq": "rope", "rk": "rope", "av": "plain", "rv": "plain"}
GAINED = ("aq", "ak", "rq", "rk")
ROW_CHUNK = 512


def _rope(y, cos, sin_up, sin_dn):
    q = HEAD_DIM // 4
    return y * cos + pltpu.roll(y, HEAD_DIM - q, 1) * sin_up + pltpu.roll(y, q, 1) * sin_dn


def _silu(y):
    h = 0.5 * y
    return h + h * jnp.tanh(h)


def _inproj_kernel(kind_ranges, n_tiles, n_steps, d_model, xg_ref, w_ref, ssq_ref, cos_ref, sup_ref, sdn_ref,
                   gain_ref, o_ref, acc_ref):
    t = pl.program_id(0)
    prev_col = lax.rem(t + (n_tiles - 1), n_tiles)
    tm, tn = acc_ref.shape

    def matmul():
        acc_ref[...] = jnp.dot(xg_ref[...], w_ref[...].astype(BF16), preferred_element_type=F32)

    def postprocess(kind):
        for c in range(tm // ROW_CHUNK):
            rows = slice(c * ROW_CHUNK, (c + 1) * ROW_CHUNK)
            r = lax.rsqrt(ssq_ref[rows, :] * (1.0 / d_model) + EPS)
            for g in range(tn // LANES):
                cols = slice(g * LANES, (g + 1) * LANES)
                y = acc_ref[rows, cols] * r
                if kind == "norm_rope":
                    ms = jnp.mean(y * y, axis=-1, keepdims=True)
                    y = _rope(y * lax.rsqrt(ms + EPS) * gain_ref[...],
                              cos_ref[rows, :], sup_ref[rows, :], sdn_ref[rows, :])
                elif kind == "rope":
                    y = _rope(y * gain_ref[...], cos_ref[rows, :], sup_ref[rows, :], sdn_ref[rows, :])
                elif kind == "gate":
                    y = _silu(y)
                o_ref[rows, cols] = y.astype(BF16)

    @pl.when(t == 0)
    def _():
        matmul()

    middle = jnp.logical_and(t > 0, t < n_steps - 1)
    for kind, lo, hi in kind_ranges:
        @pl.when(jnp.logical_and(middle, jnp.logical_and(prev_col >= lo, prev_col < hi)))
        def _(kind=kind):
            postprocess(kind)
            matmul()

    @pl.when(t == n_steps - 1)
    def _():
        postprocess(kind_ranges[-1][0])


def _inproj(xg, w, layer, ssq, tables, gains, *, seq, kind_ranges, gain_starts, tile_shifts, tn):
    m, d = xg.shape
    n = w.shape[2]
    n_tiles = n // tn
    tm = min(1024, seq)
    sb = seq // tm
    n_steps = (m // tm) * n_tiles + 1
    assert kind_ranges[-1][2] == n_tiles and tm % ROW_CHUNK == 0

    def cur(t):
        t = jnp.minimum(t, n_steps - 2)
        return t // n_tiles, lax.rem(t, n_tiles)

    def prev(t):
        t = jnp.maximum(t - 1, 0)
        return t // n_tiles, lax.rem(t, n_tiles)

    def gain_index(t):
        col = prev(t)[1]
        return (sum((col >= s).astype(jnp.int32) for s in gain_starts[1:]), 0, 0)

    def weight_index(t):
        col = cur(t)[1]
        shift = sum(jnp.where(jnp.logical_and(col >= lo, col < hi), sh, 0) for lo, hi, sh in tile_shifts)
        return (layer, 0, col + shift)

    kern = functools.partial(_inproj_kernel, kind_ranges, n_tiles, n_steps, d)
    tab = pl.BlockSpec((tm, LANES), lambda t: (lax.rem(prev(t)[0], sb), 0))
    return pl.pallas_call(
        kern,
        grid=(n_steps,),
        in_specs=[pl.BlockSpec((tm, d), lambda t: (cur(t)[0], 0)),
                  pl.BlockSpec((None, d, tn), weight_index),
                  pl.BlockSpec((tm, LANES), lambda t: (prev(t)[0], 0)),
                  tab, tab, tab,
                  pl.BlockSpec((None, 1, LANES), gain_index)],
        out_specs=pl.BlockSpec((tm, tn), lambda t: prev(t)),
        out_shape=jax.ShapeDtypeStruct((m, n), BF16),
        scratch_shapes=[pltpu.VMEM((tm, tn), F32)],
        compiler_params=_cparams("arbitrary"),
        name="inproj",
    )(xg, w, ssq, *tables, gains)


def _transpose_via_mxu(a):
    rows = lax.broadcasted_iota(jnp.int32, (LANES, LANES), 0)
    cols = lax.broadcasted_iota(jnp.int32, (LANES, LANES), 1)
    eye = jnp.where(rows == cols, 1.0, 0.0).astype(BF16)
    return lax.dot_general(eye, a, NT_DIMS, preferred_element_type=F32).astype(BF16)


SCORE_LIMIT = 50.0
Q_UNROLL = 2


def _attn_kernel(tq, tk, q_ref, k_ref, v_ref, g_ref, gains_ref, o_ref, vt_ref):
    seq = q_ref.shape[0]
    nq = ATTN_GROUP * tq
    vt_ref[...] = _transpose_via_mxu(v_ref[...])
    gain_q = jnp.max(jnp.abs(gains_ref[0:1, :]))
    gain_k = jnp.max(jnp.abs(gains_ref[1:2, :]))
    bound = gain_q * gain_k * (HEAD_DIM * 1.02)

    def load_q(qi):
        rows = pl.ds(pl.multiple_of(qi * tq, tq), tq)
        qb = q_ref[rows, :]
        qs = jnp.concatenate([qb[:, g * LANES:(g + 1) * LANES] for g in range(ATTN_GROUP)], axis=0)
        return rows, qs

    def finish(rows, o):
        for g in range(ATTN_GROUP):
            cols = slice(g * LANES, (g + 1) * LANES)
            og = o[:, g * tq:(g + 1) * tq].T
            o_ref[rows, cols] = (g_ref[rows, cols].astype(F32) * og).astype(BF16)

    def unshifted():
        def qblock(qi, carry):
            rows, qs = load_q(qi)

            def kvstep(c, st):
                l, acc = st
                c0 = pl.multiple_of(c * tk, tk)
                s = lax.dot_general(k_ref[pl.ds(c0, tk), :], qs, NT_DIMS,
                                    preferred_element_type=F32)
                p = jnp.exp2(s)
                l = l + jnp.sum(p.reshape(tk // 8, 8, nq), axis=0)
                return l, acc + jnp.dot(vt_ref[:, pl.ds(c0, tk)], p.astype(BF16), preferred_element_type=F32)

            init = (jnp.zeros((8, nq), F32), jnp.zeros((HEAD_DIM, nq), F32))
            l, acc = lax.fori_loop(0, seq // tk, kvstep, init, unroll=True)
            finish(rows, acc * (1.0 / jnp.sum(l, axis=0, keepdims=True)))
            return carry

        lax.fori_loop(0, seq // tq, qblock, 0, unroll=Q_UNROLL)

    def running_max():
        def qblock(qi, carry):
            rows, qs = load_q(qi)

            def kvstep(c, st):
                m, l, acc = st
                c0 = pl.multiple_of(c * tk, tk)
                s = lax.dot_general(k_ref[pl.ds(c0, tk), :], qs, NT_DIMS,
                                    preferred_element_type=F32)
                m_new = jnp.maximum(m, jnp.max(s, axis=0, keepdims=True))
                alpha = jnp.exp2(m - m_new)
                p = jnp.exp2(s - m_new)
                l = alpha * l + jnp.sum(p, axis=0, keepdims=True)
                pv = jnp.dot(vt_ref[:, pl.ds(c0, tk)], p.astype(BF16), preferred_element_type=F32)
                return m_new, l, alpha * acc + pv

            init = (jnp.full((1, nq), -1e30, F32), jnp.zeros((1, nq), F32), jnp.zeros((HEAD_DIM, nq), F32))
            _, l, acc = lax.fori_loop(0, seq // tk, kvstep, init)
            finish(rows, acc * (1.0 / l))
            return carry

        lax.fori_loop(0, seq // tq, qblock, 0)

    lax.cond(bound <= SCORE_LIMIT, unshifted, running_max)


def _attention(proj, gains, *, batch, seq, attn_width, kv_width, offs):
    kvh = kv_width // HEAD_DIM
    gw = ATTN_GROUP * HEAD_DIM
    tq = min(256, seq)
    tk = min(1024, seq)
    assert offs["aq"] % gw == 0 and offs["ag"] % gw == 0
    q_off = offs["aq"] // gw
    k_off = offs["ak"] // HEAD_DIM
    v_off = offs["av"] // HEAD_DIM
    g_off = offs["ag"] // gw
    return pl.pallas_call(
        functools.partial(_attn_kernel, tq, tk),
        grid=(batch, kvh),
        in_specs=[pl.BlockSpec((seq, gw), lambda b, h: (b, q_off + h)),
                  pl.BlockSpec((seq, HEAD_DIM), lambda b, h: (b, k_off + h)),
                  pl.BlockSpec((seq, HEAD_DIM), lambda b, h: (b, v_off + h)),
                  pl.BlockSpec((seq, gw), lambda b, h: (b, g_off + h)),
                  pl.BlockSpec(gains.shape, lambda b, h: (0, 0))],
        out_specs=pl.BlockSpec((seq, gw), lambda b, h: (b, h)),
        out_shape=jax.ShapeDtypeStruct((batch * seq, attn_width), BF16),
        scratch_shapes=[pltpu.VMEM((HEAD_DIM, seq), BF16)],
        compiler_params=_cparams("parallel", "parallel"),
        name="attention",
    )(proj, proj, proj, proj, gains)


RET_UNROLL = 16
RET_HEADS_PER_STEP = 2


def _log_sigmoid(x):
    return jnp.minimum(x, 0.0) - jnp.log1p(jnp.exp(-jnp.abs(x)))


def _retention_kernel(ck, q_ref, k_ref, v_ref, g_ref, df_ref, db_ref, gn_ref, o_ref, kt_ref, sb_ref):
    seq = q_ref.shape[0]
    nchunk = seq // ck
    heads = range(RET_HEADS_PER_STEP)
    qk = lambda e: slice(e * RET_QK_DIM, (e + 1) * RET_QK_DIM)
    vv = lambda e: slice(e * RET_V_DIM, (e + 1) * RET_V_DIM)

    diff = (lax.broadcasted_iota(jnp.int32, (ck, ck), 0)
            - lax.broadcasted_iota(jnp.int32, (ck, ck), 1)).astype(F32)
    rowi = lax.broadcasted_iota(jnp.int32, (ck, RET_QK_DIM), 0).astype(F32)
    coli = lax.broadcasted_iota(jnp.int32, (RET_QK_DIM, ck), 1).astype(F32)
    decay, xi_f, xi_b, zt_f, zt_b, gc_f, gc_b = [], [], [], [], [], [], []
    for e in heads:
        lgf = _log_sigmoid(df_ref[e])[:, :1]
        lgb = _log_sigmoid(db_ref[e])[:, :1]
        kt_ref[e] = _transpose_via_mxu(k_ref[:, qk(e)])
        decay.append(jnp.exp(jnp.where(diff >= 0, lgf * diff, -lgb * diff)))
        xi_f.append(jnp.exp(lgf * (rowi + 1.0)))
        xi_b.append(jnp.exp(lgb * (ck - rowi)))
        zt_f.append(jnp.exp(lgf * (ck - 1.0 - coli)))
        zt_b.append(jnp.exp(lgb * coli))
        gc_f.append(jnp.exp(lgf * ck))
        gc_b.append(jnp.exp(lgb * ck))

    def kz(e, c0, zt):
        return (kt_ref[e, :, pl.ds(c0, ck)].astype(F32) * zt).astype(BF16)

    def bstep(t, states):
        jc = nchunk - 1 - t
        c0 = pl.multiple_of(jc * ck, ck)
        new = []
        for e in heads:
            sb_ref[e, jc] = states[e].astype(BF16)
            upd = jnp.dot(kz(e, c0, zt_b[e]), v_ref[pl.ds(c0, ck), vv(e)], preferred_element_type=F32)
            new.append(gc_b[e] * states[e] + upd)
        return tuple(new)

    zeros = tuple(jnp.zeros((RET_QK_DIM, RET_V_DIM), F32) for _ in heads)
    lax.fori_loop(0, nchunk, bstep, zeros, unroll=RET_UNROLL)

    def fstep(jc, states):
        c0 = pl.multiple_of(jc * ck, ck)
        rows = pl.ds(c0, ck)
        new = []
        for e in heads:
            q = q_ref[rows, qk(e)]
            v = v_ref[rows, vv(e)]
            s = jnp.dot(q, kt_ref[e, :, pl.ds(c0, ck)], preferred_element_type=F32)
            out = jnp.dot((s * decay[e]).astype(BF16), v, preferred_element_type=F32)
            qf = q.astype(F32)
            qx = jnp.concatenate([(qf * xi_f[e]).astype(BF16), (qf * xi_b[e]).astype(BF16)], axis=1)
            both = jnp.concatenate([states[e].astype(BF16), sb_ref[e, jc]], axis=0)
            out = out + jnp.dot(qx, both, preferred_element_type=F32)
            ms = jnp.mean(out * out, axis=-1, keepdims=True)
            y = out * lax.rsqrt(ms + EPS) * gn_ref[e]
            o_ref[rows, vv(e)] = (g_ref[rows, vv(e)].astype(F32) * y).astype(BF16)
            upd = jnp.dot(kz(e, c0, zt_f[e]), v, preferred_element_type=F32)
            new.append(gc_f[e] * states[e] + upd)
        return tuple(new)

    lax.fori_loop(0, nchunk, fstep, zeros, unroll=RET_UNROLL)


def _retention(proj, dfw, dbw, gnorm, *, batch, seq, offs, ret_heads):
    hps = RET_HEADS_PER_STEP
    wq, wv = hps * RET_QK_DIM, hps * RET_V_DIM
    off_q, off_k, off_v, off_g = offs["rq"], offs["rk"], offs["rv"], offs["rg"]
    assert ret_heads % hps == 0 and off_q % wq == 0 and off_k % wq == 0 and off_v % wv == 0 and off_g % wv == 0
    ck = min(256, seq)
    vec = lambda w: pl.BlockSpec((hps, 1, w), lambda b, h: (h, 0, 0))
    return pl.pallas_call(
        functools.partial(_retention_kernel, ck),
        grid=(batch, ret_heads // hps),
        in_specs=[pl.BlockSpec((seq, wq), lambda b, h: (b, off_q // wq + h)),
                  pl.BlockSpec((seq, wq), lambda b, h: (b, off_k // wq + h)),
                  pl.BlockSpec((seq, wv), lambda b, h: (b, off_v // wv + h)),
                  pl.BlockSpec((seq, wv), lambda b, h: (b, off_g // wv + h)),
                  vec(LANES), vec(LANES), vec(RET_V_DIM)],
        out_specs=pl.BlockSpec((seq, wv), lambda b, h: (b, h)),
        out_shape=jax.ShapeDtypeStruct((batch * seq, ret_heads * RET_V_DIM), BF16),
        scratch_shapes=[pltpu.VMEM((hps, RET_QK_DIM, seq), BF16),
                        pltpu.VMEM((hps, seq // ck, RET_QK_DIM, RET_V_DIM), BF16)],
        compiler_params=_cparams("parallel", "parallel"),
        name="retention",
    )(proj, proj, proj, proj, dfw, dbw, gnorm)


def _outproj_kernel(n_tiles, n_steps, a_ref, r_ref, wa_ref, wr_ref, x_ref, g_ref, xo_ref, xg_ref, ssq_ref, acc_ref):
    t = pl.program_id(0)
    prev_col = lax.rem(t + (n_tiles - 1), n_tiles)

    def matmul():
        acc = jnp.dot(a_ref[...], wa_ref[...], preferred_element_type=F32)
        acc_ref[...] = acc + jnp.dot(r_ref[...], wr_ref[...], preferred_element_type=F32)

    def finish():
        xn = x_ref[...] + acc_ref[...]
        xo_ref[...] = xn
        xg_ref[...] = (xn * g_ref[...]).astype(BF16)
        return jnp.broadcast_to(jnp.sum(xn * xn, axis=-1, keepdims=True), ssq_ref.shape)

    def accumulate(part):
        @pl.when(prev_col == 0)
        def _():
            ssq_ref[...] = part

        @pl.when(prev_col > 0)
        def _():
            ssq_ref[...] = ssq_ref[...] + part

    @pl.when(t == 0)
    def _():
        matmul()

    @pl.when(jnp.logical_and(t > 0, t < n_steps - 1))
    def _():
        part = finish()
        matmul()
        accumulate(part)

    @pl.when(t == n_steps - 1)
    def _():
        accumulate(finish())


def _outproj(a, r, w, layer, x2, gnext):
    m, d = x2.shape
    ka, kr = a.shape[1], r.shape[1]
    assert ka == kr and w.shape[1] == ka + kr
    tm = min(1024, m)
    tn = min(512, d)
    n_tiles = d // tn
    n_steps = (m // tm) * n_tiles + 1

    def cur(t):
        t = jnp.minimum(t, n_steps - 2)
        return t // n_tiles, lax.rem(t, n_tiles)

    def prev(t):
        t = jnp.maximum(t - 1, 0)
        return t // n_tiles, lax.rem(t, n_tiles)

    return pl.pallas_call(
        functools.partial(_outproj_kernel, n_tiles, n_steps),
        grid=(n_steps,),
        in_specs=[pl.BlockSpec((tm, ka), lambda t: (cur(t)[0], 0)),
                  pl.BlockSpec((tm, kr), lambda t: (cur(t)[0], 0)),
                  pl.BlockSpec((None, ka, tn), lambda t: (layer, 0, cur(t)[1])),
                  pl.BlockSpec((None, kr, tn), lambda t: (layer, 1, cur(t)[1])),
                  pl.BlockSpec((tm, tn), lambda t: prev(t)),
                  pl.BlockSpec((1, tn), lambda t: (0, prev(t)[1]))],
        out_specs=[pl.BlockSpec((tm, tn), lambda t: prev(t)),
                   pl.BlockSpec((tm, tn), lambda t: prev(t)),
                   pl.BlockSpec((tm, LANES), lambda t: (prev(t)[0], 0))],
        out_shape=[jax.ShapeDtypeStruct((m, d), F32),
                   jax.ShapeDtypeStruct((m, d), BF16),
                   jax.ShapeDtypeStruct((m, LANES), F32)],
        scratch_shapes=[pltpu.VMEM((tm, tn), F32)],
        compiler_params=_cparams("arbitrary"),
        name="outproj",
    )(a, r, w, w, x2, gnext)


FINAL_TM = 512


def _outproj_final_kernel(n_tiles, n_mm, d_model, a_ref, r_ref, wa_ref, wr_ref, x_ref, g_ref, o_ref,
                          acc_ref, xrow_ref, ssq_ref):
    t = pl.program_id(0)
    fin = jnp.maximum(t - 1, 0)
    fin_row, fin_col = fin // n_tiles, lax.rem(fin, n_tiles)
    emi = jnp.maximum(t - 1 - n_tiles, 0)
    emi_row, emi_col = emi // n_tiles, lax.rem(emi, n_tiles)

    def matmul():
        acc = jnp.dot(a_ref[...], wa_ref[...], preferred_element_type=F32)
        acc_ref[...] = acc + jnp.dot(r_ref[...], wr_ref[...], preferred_element_type=F32)

    def finish():
        half = lax.rem(fin_row, 2)
        xn = x_ref[...] + acc_ref[...]
        xrow_ref[half * n_tiles + fin_col] = xn
        part = jnp.broadcast_to(jnp.sum(xn * xn, axis=-1, keepdims=True), ssq_ref.shape[1:])
        ssq_ref[half] = ssq_ref[half] + part

    def emit():
        half = lax.rem(emi_row, 2)
        ssq = ssq_ref[half]
        scale = lax.rsqrt(ssq[:, :1] * (1.0 / d_model) + EPS)
        o_ref[...] = xrow_ref[half * n_tiles + emi_col] * scale * g_ref[...]
        ssq_ref[half] = ssq * jnp.where(emi_col == n_tiles - 1, 0.0, 1.0)

    @pl.when(t == 0)
    def _():
        ssq_ref[...] = jnp.zeros(ssq_ref.shape, F32)
        matmul()

    @pl.when(jnp.logical_and(t >= 1, t <= n_tiles))
    def _():
        finish()
        matmul()

    @pl.when(jnp.logical_and(t > n_tiles, t < n_mm))
    def _():
        finish()
        emit()
        matmul()

    @pl.when(t == n_mm)
    def _():
        finish()
        emit()

    @pl.when(t > n_mm)
    def _():
        emit()


def _outproj_final(a, r, w, layer, x2, g):
    m, d = x2.shape
    ka, kr = a.shape[1], r.shape[1]
    assert ka == kr and w.shape[1] == ka + kr
    tm = min(FINAL_TM, m)
    tn = min(512, d)
    n_tiles = d // tn
    n_mm = (m // tm) * n_tiles
    n_steps = n_mm + n_tiles + 1
    assert n_mm > n_tiles

    def cur(t):
        t = jnp.minimum(t, n_mm - 1)
        return t // n_tiles, lax.rem(t, n_tiles)

    def fin(t):
        t = jnp.clip(t - 1, 0, n_mm - 1)
        return t // n_tiles, lax.rem(t, n_tiles)

    def emi(t):
        t = jnp.maximum(t - 1 - n_tiles, 0)
        return t // n_tiles, lax.rem(t, n_tiles)

    return pl.pallas_call(
        functools.partial(_outproj_final_kernel, n_tiles, n_mm, d),
        grid=(n_steps,),
        in_specs=[pl.BlockSpec((tm, ka), lambda t: (cur(t)[0], 0)),
                  pl.BlockSpec((tm, kr), lambda t: (cur(t)[0], 0)),
                  pl.BlockSpec((None, ka, tn), lambda t: (layer, 0, cur(t)[1])),
                  pl.BlockSpec((None, kr, tn), lambda t: (layer, 1, cur(t)[1])),
                  pl.BlockSpec((tm, tn), lambda t: fin(t)),
                  pl.BlockSpec((1, tn), lambda t: (0, emi(t)[1]))],
        out_specs=pl.BlockSpec((tm, tn), lambda t: emi(t)),
        out_shape=jax.ShapeDtypeStruct((m, d), F32),
        scratch_shapes=[pltpu.VMEM((tm, tn), F32),
                        pltpu.VMEM((2 * n_tiles, tm, tn), F32),
                        pltpu.VMEM((2, tm, LANES), F32)],
        compiler_params=_cparams("arbitrary"),
        name="outproj_final",
    )(a, r, w, w, x2, g)


def _rope_tables(seq):
    rows = seq // GRID_W
    row = jnp.repeat(jnp.arange(rows), GRID_W).astype(F32)
    col = jnp.tile(jnp.arange(GRID_W), rows).astype(F32)
    axis_dim = HEAD_DIM // 2
    inv = ROPE_THETA ** (-jnp.arange(0, axis_dim, 2, dtype=F32) / axis_dim)
    ang_r = row[:, None] * inv[None, :]
    ang_c = col[:, None] * inv[None, :]
    cr, sr, cc, sc = jnp.cos(ang_r), jnp.sin(ang_r), jnp.cos(ang_c), jnp.sin(ang_c)
    z = jnp.zeros_like(sr)
    cos = jnp.concatenate([cr, cr, cc, cc], axis=-1)
    sin_up = jnp.concatenate([-sr, z, -sc, z], axis=-1)
    sin_dn = jnp.concatenate([z, sr, z, sc], axis=-1)
    return cos, sin_up, sin_dn


def kernel(x, norm_w, w_in, q_norm, k_norm, ret_decay_fwd, ret_decay_bwd, ret_norm, w_out, final_norm):
    batch, seq, d = x.shape
    depth = w_in.shape[0]
    attn_width = d // 2
    kv_width = attn_width // ATTN_GROUP
    ret_width = d - attn_width
    ret_heads = ret_width // RET_V_DIM
    ret_qk_width = ret_heads * RET_QK_DIM
    m = batch * seq

    ref_widths = (("aq", attn_width), ("ak", kv_width), ("av", kv_width), ("ag", attn_width),
                  ("rq", ret_qk_width), ("rk", ret_qk_width), ("rv", ret_width), ("rg", ret_width))
    ref_pos, pos = {}, 0
    for name, wdt in ref_widths:
        ref_pos[name] = (pos, wdt)
        pos += wdt
    assert pos == w_in.shape[2]
    tn = min(512, kv_width)
    offs, kind_ranges, gain_starts, pos = {}, [], [], 0
    for name in SECTION_ORDER:
        wdt = ref_pos[name][1]
        assert wdt % tn == 0
        offs[name] = pos
        kind = SECTION_KIND[name]
        if kind_ranges and kind_ranges[-1][0] == kind:
            kind_ranges[-1] = (kind, kind_ranges[-1][1], (pos + wdt) // tn)
        else:
            kind_ranges.append((kind, pos // tn, (pos + wdt) // tn))
        if name in GAINED:
            gain_starts.append(pos // tn)
        pos += wdt

    tile_shifts = tuple((offs[name] // tn, (offs[name] + ref_pos[name][1]) // tn, (ref_pos[name][0] - offs[name]) // tn)
                        for name in SECTION_ORDER)
    w_out_b = w_out.astype(BF16)
    tables = _rope_tables(seq)
    attn_scale = HEAD_DIM ** -0.5
    x2 = x.reshape(m, d)

    xg, ssq = _prologue(x2, norm_w[0].reshape(1, d))
    for l in range(depth):
        gains = jnp.stack([q_norm[l] * (attn_scale * LOG2E),
                           k_norm[l],
                           jnp.ones((HEAD_DIM,), F32),
                           jnp.full((HEAD_DIM,), RET_QK_DIM ** -0.5, F32)]).reshape(len(GAINED), 1, HEAD_DIM)
        proj = _inproj(xg, w_in, l, ssq, tables, gains, seq=seq, kind_ranges=tuple(kind_ranges),
                       gain_starts=tuple(gain_starts), tile_shifts=tile_shifts, tn=tn)
        a = _attention(proj, gains.reshape(len(GAINED), HEAD_DIM), batch=batch, seq=seq, attn_width=attn_width, kv_width=kv_width, offs=offs)
        dfw = jnp.broadcast_to(ret_decay_fwd[l][:, None, None], (ret_heads, 1, LANES))
        dbw = jnp.broadcast_to(ret_decay_bwd[l][:, None, None], (ret_heads, 1, LANES))
        r = _retention(proj, dfw, dbw, ret_norm[l].reshape(ret_heads, 1, RET_V_DIM),
                       batch=batch, seq=seq, offs=offs, ret_heads=ret_heads)
        if l + 1 < depth:
            x2, xg, ssq = _outproj(a, r, w_out_b, l, x2, norm_w[l + 1].reshape(1, d))
        else:
            out = _outproj_final(a, r, w_out_b, l, x2, final_norm.reshape(1, d))
    return out.reshape(batch, seq, d)
```

```python
import functools
import math

import jax
import jax.numpy as jnp
from jax import lax
from jax.experimental import pallas as pl
from jax.experimental.pallas import tpu as pltpu

F32 = jnp.float32
BF16 = jnp.bfloat16

GRID_W = 64
HEAD_DIM = 128
ATTN_GROUP = 4
RET_V_DIM = 256
RET_QK_DIM = 128
ROPE_THETA = 10000.0
EPS = 1e-6
LOG2E = math.log2(math.e)

LANES = 128
VMEM_LIMIT = 56 * 1024 * 1024

NT_DIMS = (((1,), (1,)), ((), ()))


def _cparams(*sem):
    return pltpu.CompilerParams(dimension_semantics=sem, vmem_limit_bytes=VMEM_LIMIT)


def _prologue_kernel(x_ref, g_ref, xg_ref, ssq_ref):
    x = x_ref[...]
    xg_ref[...] = (x * g_ref[...]).astype(BF16)
    ssq_ref[...] = jnp.broadcast_to(jnp.sum(x * x, axis=-1, keepdims=True), ssq_ref.shape)


def _prologue(x2, g):
    m, d = x2.shape
    tm = min(256, m)
    return pl.pallas_call(
        _prologue_kernel,
        grid=(m // tm,),
        in_specs=[pl.BlockSpec((tm, d), lambda i: (i, 0)),
                  pl.BlockSpec((1, d), lambda i: (0, 0))],
        out_specs=[pl.BlockSpec((tm, d), lambda i: (i, 0)),
                   pl.BlockSpec((tm, LANES), lambda i: (i, 0))],
        out_shape=[jax.ShapeDtypeStruct((m, d), BF16),
                   jax.ShapeDtypeStruct((m, LANES), F32)],
        compiler_params=_cparams("parallel"),
        name="prologue",
    )(x2, g)


SECTION_ORDER = ("ag", "rg", "aq", "ak", "rq", "rk", "av", "rv")
SECTION_KIND = {"ag": "gate", "rg": "gate", "aq": "norm_rope", "ak": "norm_rope",
                "rq": "rope", "rk": "rope", "av": "plain", "rv": "plain"}
GAINED = ("aq", "ak", "rq", "rk")
ROW_CHUNK = 512


def _rope(y, cos, sin_up, sin_dn):
    q = HEAD_DIM // 4
    return y * cos + pltpu.roll(y, HEAD_DIM - q, 1) * sin_up + pltpu.roll(y, q, 1) * sin_dn


def _silu(y):
    h = 0.5 * y
    return h + h * jnp.tanh(h)


def _inproj_kernel(kind_ranges, n_tiles, n_steps, d_model, xg_ref, w_ref, ssq_ref, cos_ref, sup_ref, sdn_ref,
                   gain_ref, o_ref, acc_ref):
    t = pl.program_id(0)
    prev_col = lax.rem(t + (n_tiles - 1), n_tiles)
    tm, tn = acc_ref.shape

    def matmul():
        acc_ref[...] = jnp.dot(xg_ref[...], w_ref[...].astype(BF16), preferred_element_type=F32)

    def postprocess(kind):
        for c in range(tm // ROW_CHUNK):
            rows = slice(c * ROW_CHUNK, (c + 1) * ROW_CHUNK)
            r = lax.rsqrt(ssq_ref[rows, :] * (1.0 / d_model) + EPS)
            for g in range(tn // LANES):
                cols = slice(g * LANES, (g + 1) * LANES)
                y = acc_ref[rows, cols] * r
                if kind == "norm_rope":
                    ms = jnp.mean(y * y, axis=-1, keepdims=True)
                    y = _rope(y * lax.rsqrt(ms + EPS) * gain_ref[...],
                              cos_ref[rows, :], sup_ref[rows, :], sdn_ref[rows, :])
                elif kind == "rope":
                    y = _rope(y * gain_ref[...], cos_ref[rows, :], sup_ref[rows, :], sdn_ref[rows, :])
                elif kind == "gate":
                    y = _silu(y)
                o_ref[rows, cols] = y.astype(BF16)

    @pl.when(t == 0)
    def _():
        matmul()

    middle = jnp.logical_and(t > 0, t < n_steps - 1)
    for kind, lo, hi in kind_ranges:
        @pl.when(jnp.logical_and(middle, jnp.logical_and(prev_col >= lo, prev_col < hi)))
        def _(kind=kind):
            postprocess(kind)
            matmul()

    @pl.when(t == n_steps - 1)
    def _():
        postprocess(kind_ranges[-1][0])


def _inproj(xg, w, layer, ssq, tables, gains, *, seq, kind_ranges, gain_starts, tile_shifts, tn):
    m, d = xg.shape
    n = w.shape[2]
    n_tiles = n // tn
    tm = min(1024, seq)
    sb = seq // tm
    n_steps = (m // tm) * n_tiles + 1
    assert kind_ranges[-1][2] == n_tiles and tm % ROW_CHUNK == 0

    def cur(t):
        t = jnp.minimum(t, n_steps - 2)
        return t // n_tiles, lax.rem(t, n_tiles)

    def prev(t):
        t = jnp.maximum(t - 1, 0)
        return t // n_tiles, lax.rem(t, n_tiles)

    def gain_index(t):
        col = prev(t)[1]
        return (sum((col >= s).astype(jnp.int32) for s in gain_starts[1:]), 0, 0)

    def weight_index(t):
        col = cur(t)[1]
        shift = sum(jnp.where(jnp.logical_and(col >= lo, col < hi), sh, 0) for lo, hi, sh in tile_shifts)
        return (layer, 0, col + shift)

    kern = functools.partial(_inproj_kernel, kind_ranges, n_tiles, n_steps, d)
    tab = pl.BlockSpec((tm, LANES), lambda t: (lax.rem(prev(t)[0], sb), 0))
    return pl.pallas_call(
        kern,
        grid=(n_steps,),
        in_specs=[pl.BlockSpec((tm, d), lambda t: (cur(t)[0], 0)),
                  pl.BlockSpec((None, d, tn), weight_index),
                  pl.BlockSpec((tm, LANES), lambda t: (prev(t)[0], 0)),
                  tab, tab, tab,
                  pl.BlockSpec((None, 1, LANES), gain_index)],
        out_specs=pl.BlockSpec((tm, tn), lambda t: prev(t)),
        out_shape=jax.ShapeDtypeStruct((m, n), BF16),
        scratch_shapes=[pltpu.VMEM((tm, tn), F32)],
        compiler_params=_cparams("arbitrary"),
        name="inproj",
    )(xg, w, ssq, *tables, gains)


def _transpose_via_mxu(a):
    rows = lax.broadcasted_iota(jnp.int32, (LANES, LANES), 0)
    cols = lax.broadcasted_iota(jnp.int32, (LANES, LANES), 1)
    eye = jnp.where(rows == cols, 1.0, 0.0).astype(BF16)
    return lax.dot_general(eye, a, NT_DIMS, preferred_element_type=F32).astype(BF16)


SCORE_LIMIT = 50.0
Q_UNROLL = 4


def _attn_kernel(tq, tk, q_ref, k_ref, v_ref, g_ref, gains_ref, o_ref, vt_ref):
    seq = q_ref.shape[0]
    nq = ATTN_GROUP * tq
    vt_ref[...] = _transpose_via_mxu(v_ref[...])
    gain_q = jnp.max(jnp.abs(gains_ref[0:1, :]))
    gain_k = jnp.max(jnp.abs(gains_ref[1:2, :]))
    bound = gain_q * gain_k * (HEAD_DIM * 1.02)

    def load_q(qi):
        rows = pl.ds(pl.multiple_of(qi * tq, tq), tq)
        qb = q_ref[rows, :]
        qs = jnp.concatenate([qb[:, g * LANES:(g + 1) * LANES] for g in range(ATTN_GROUP)], axis=0)
        return rows, qs

    def finish(rows, o):
        for g in range(ATTN_GROUP):
            cols = slice(g * LANES, (g + 1) * LANES)
            og = o[:, g * tq:(g + 1) * tq].T
            o_ref[rows, cols] = (g_ref[rows, cols].astype(F32) * og).astype(BF16)

    def unshifted():
        def qblock(qi, carry):
            rows, qs = load_q(qi)

            def kvstep(c, st):
                l, acc = st
                c0 = pl.multiple_of(c * tk, tk)
                s = lax.dot_general(k_ref[pl.ds(c0, tk), :], qs, NT_DIMS,
                                    preferred_element_type=F32)
                p = jnp.exp2(s)
                l = l + jnp.sum(p.reshape(tk // 8, 8, nq), axis=0)
                return l, acc + jnp.dot(vt_ref[:, pl.ds(c0, tk)], p.astype(BF16), preferred_element_type=F32)

            init = (jnp.zeros((8, nq), F32), jnp.zeros((HEAD_DIM, nq), F32))
            l, acc = lax.fori_loop(0, seq // tk, kvstep, init, unroll=True)
            finish(rows, acc * (1.0 / jnp.sum(l, axis=0, keepdims=True)))
            return carry

        lax.fori_loop(0, seq // tq, qblock, 0, unroll=Q_UNROLL)

    def running_max():
        def qblock(qi, carry):
            rows, qs = load_q(qi)

            def kvstep(c, st):
                m, l, acc = st
                c0 = pl.multiple_of(c * tk, tk)
                s = lax.dot_general(k_ref[pl.ds(c0, tk), :], qs, NT_DIMS,
                                    preferred_element_type=F32)
                m_new = jnp.maximum(m, jnp.max(s, axis=0, keepdims=True))
                alpha = jnp.exp2(m - m_new)
                p = jnp.exp2(s - m_new)
                l = alpha * l + jnp.sum(p, axis=0, keepdims=True)
                pv = jnp.dot(vt_ref[:, pl.ds(c0, tk)], p.astype(BF16), preferred_element_type=F32)
                return m_new, l, alpha * acc + pv

            init = (jnp.full((1, nq), -1e30, F32), jnp.zeros((1, nq), F32), jnp.zeros((HEAD_DIM, nq), F32))
            _, l, acc = lax.fori_loop(0, seq // tk, kvstep, init)
            finish(rows, acc * (1.0 / l))
            return carry

        lax.fori_loop(0, seq // tq, qblock, 0)

    lax.cond(bound <= SCORE_LIMIT, unshifted, running_max)


def _attention(proj, gains, *, batch, seq, attn_width, kv_width, offs):
    kvh = kv_width // HEAD_DIM
    gw = ATTN_GROUP * HEAD_DIM
    tq = min(256, seq)
    tk = min(1024, seq)
    assert offs["aq"] % gw == 0 and offs["ag"] % gw == 0
    q_off = offs["aq"] // gw
    k_off = offs["ak"] // HEAD_DIM
    v_off = offs["av"] // HEAD_DIM
    g_off = offs["ag"] // gw
    return pl.pallas_call(
        functools.partial(_attn_kernel, tq, tk),
        grid=(batch, kvh),
        in_specs=[pl.BlockSpec((seq, gw), lambda b, h: (b, q_off + h)),
                  pl.BlockSpec((seq, HEAD_DIM), lambda b, h: (b, k_off + h)),
                  pl.BlockSpec((seq, HEAD_DIM), lambda b, h: (b, v_off + h)),
                  pl.BlockSpec((seq, gw), lambda b, h: (b, g_off + h)),
                  pl.BlockSpec(gains.shape, lambda b, h: (0, 0))],
        out_specs=pl.BlockSpec((seq, gw), lambda b, h: (b, h)),
        out_shape=jax.ShapeDtypeStruct((batch * seq, attn_width), BF16),
        scratch_shapes=[pltpu.VMEM((HEAD_DIM, seq), BF16)],
        compiler_params=_cparams("parallel", "parallel"),
        name="attention",
    )(proj, proj, proj, proj, gains)


RET_UNROLL = 16
RET_HEADS_PER_STEP = 2


def _log_sigmoid(x):
    return jnp.minimum(x, 0.0) - jnp.log1p(jnp.exp(-jnp.abs(x)))


def _retention_kernel(ck, q_ref, k_ref, v_ref, g_ref, df_ref, db_ref, gn_ref, o_ref, kt_ref, sb_ref):
    seq = q_ref.shape[0]
    nchunk = seq // ck
    heads = range(RET_HEADS_PER_STEP)
    qk = lambda e: slice(e * RET_QK_DIM, (e + 1) * RET_QK_DIM)
    vv = lambda e: slice(e * RET_V_DIM, (e + 1) * RET_V_DIM)

    diff = (lax.broadcasted_iota(jnp.int32, (ck, ck), 0)
            - lax.broadcasted_iota(jnp.int32, (ck, ck), 1)).astype(F32)
    rowi = lax.broadcasted_iota(jnp.int32, (ck, RET_QK_DIM), 0).astype(F32)
    coli = lax.broadcasted_iota(jnp.int32, (RET_QK_DIM, ck), 1).astype(F32)
    decay, xi_f, xi_b, zt_f, zt_b, gc_f, gc_b = [], [], [], [], [], [], []
    for e in heads:
        lgf = _log_sigmoid(df_ref[e])[:, :1]
        lgb = _log_sigmoid(db_ref[e])[:, :1]
        kt_ref[e] = _transpose_via_mxu(k_ref[:, qk(e)])
        decay.append(jnp.exp(jnp.where(diff >= 0, lgf * diff, -lgb * diff)))
        xi_f.append(jnp.exp(lgf * (rowi + 1.0)))
        xi_b.append(jnp.exp(lgb * (ck - rowi)))
        zt_f.append(jnp.exp(lgf * (ck - 1.0 - coli)))
        zt_b.append(jnp.exp(lgb * coli))
        gc_f.append(jnp.exp(lgf * ck))
        gc_b.append(jnp.exp(lgb * ck))

    def kz(e, c0, zt):
        return (kt_ref[e, :, pl.ds(c0, ck)].astype(F32) * zt).astype(BF16)

    def bstep(t, states):
        jc = nchunk - 1 - t
        c0 = pl.multiple_of(jc * ck, ck)
        new = []
        for e in heads:
            sb_ref[e, jc] = states[e].astype(BF16)
            upd = jnp.dot(kz(e, c0, zt_b[e]), v_ref[pl.ds(c0, ck), vv(e)], preferred_element_type=F32)
            new.append(gc_b[e] * states[e] + upd)
        return tuple(new)

    zeros = tuple(jnp.zeros((RET_QK_DIM, RET_V_DIM), F32) for _ in heads)
    lax.fori_loop(0, nchunk, bstep, zeros, unroll=RET_UNROLL)

    def fstep(jc, states):
        c0 = pl.multiple_of(jc * ck, ck)
        rows = pl.ds(c0, ck)
        new = []
        for e in heads:
            q = q_ref[rows, qk(e)]
            v = v_ref[rows, vv(e)]
            s = jnp.dot(q, kt_ref[e, :, pl.ds(c0, ck)], preferred_element_type=F32)
            out = jnp.dot((s * decay[e]).astype(BF16), v, preferred_element_type=F32)
            qf = q.astype(F32)
            qx = jnp.concatenate([(qf * xi_f[e]).astype(BF16), (qf * xi_b[e]).astype(BF16)], axis=1)
            both = jnp.concatenate([states[e].astype(BF16), sb_ref[e, jc]], axis=0)
            out = out + jnp.dot(qx, both, preferred_element_type=F32)
            ms = jnp.mean(out * out, axis=-1, keepdims=True)
            y = out * lax.rsqrt(ms + EPS) * gn_ref[e]
            o_ref[rows, vv(e)] = (g_ref[rows, vv(e)].astype(F32) * y).astype(BF16)
            upd = jnp.dot(kz(e, c0, zt_f[e]), v, preferred_element_type=F32)
            new.append(gc_f[e] * states[e] + upd)
        return tuple(new)

    lax.fori_loop(0, nchunk, fstep, zeros, unroll=RET_UNROLL)


def _retention(proj, dfw, dbw, gnorm, *, batch, seq, offs, ret_heads):
    hps = RET_HEADS_PER_STEP
    wq, wv = hps * RET_QK_DIM, hps * RET_V_DIM
    off_q, off_k, off_v, off_g = offs["rq"], offs["rk"], offs["rv"], offs["rg"]
    assert ret_heads % hps == 0 and off_q % wq == 0 and off_k % wq == 0 and off_v % wv == 0 and off_g % wv == 0
    ck = min(256, seq)
    vec = lambda w: pl.BlockSpec((hps, 1, w), lambda b, h: (h, 0, 0))
    return pl.pallas_call(
        functools.partial(_retention_kernel, ck),
        grid=(batch, ret_heads // hps),
        in_specs=[pl.BlockSpec((seq, wq), lambda b, h: (b, off_q // wq + h)),
                  pl.BlockSpec((seq, wq), lambda b, h: (b, off_k // wq + h)),
                  pl.BlockSpec((seq, wv), lambda b, h: (b, off_v // wv + h)),
                  pl.BlockSpec((seq, wv), lambda b, h: (b, off_g // wv + h)),
                  vec(LANES), vec(LANES), vec(RET_V_DIM)],
        out_specs=pl.BlockSpec((seq, wv), lambda b, h: (b, h)),
        out_shape=jax.ShapeDtypeStruct((batch * seq, ret_heads * RET_V_DIM), BF16),
        scratch_shapes=[pltpu.VMEM((hps, RET_QK_DIM, seq), BF16),
                        pltpu.VMEM((hps, seq // ck, RET_QK_DIM, RET_V_DIM), BF16)],
        compiler_params=_cparams("parallel", "parallel"),
        name="retention",
    )(proj, proj, proj, proj, dfw, dbw, gnorm)


def _outproj_kernel(n_tiles, n_steps, a_ref, r_ref, wa_ref, wr_ref, x_ref, g_ref, xo_ref, xg_ref, ssq_ref, acc_ref):
    t = pl.program_id(0)
    prev_col = lax.rem(t + (n_tiles - 1), n_tiles)

    def matmul():
        acc = jnp.dot(a_ref[...], wa_ref[...], preferred_element_type=F32)
        acc_ref[...] = acc + jnp.dot(r_ref[...], wr_ref[...], preferred_element_type=F32)

    def finish():
        xn = x_ref[...] + acc_ref[...]
        xo_ref[...] = xn
        xg_ref[...] = (xn * g_ref[...]).astype(BF16)
        return jnp.broadcast_to(jnp.sum(xn * xn, axis=-1, keepdims=True), ssq_ref.shape)

    def accumulate(part):
        @pl.when(prev_col == 0)
        def _():
            ssq_ref[...] = part

        @pl.when(prev_col > 0)
        def _():
            ssq_ref[...] = ssq_ref[...] + part

    @pl.when(t == 0)
    def _():
        matmul()

    @pl.when(jnp.logical_and(t > 0, t < n_steps - 1))
    def _():
        part = finish()
        matmul()
        accumulate(part)

    @pl.when(t == n_steps - 1)
    def _():
        accumulate(finish())


def _outproj(a, r, w, layer, x2, gnext):
    m, d = x2.shape
    ka, kr = a.shape[1], r.shape[1]
    assert ka == kr and w.shape[1] == ka + kr
    tm = min(1024, m)
    tn = min(512, d)
    n_tiles = d // tn
    n_steps = (m // tm) * n_tiles + 1

    def cur(t):
        t = jnp.minimum(t, n_steps - 2)
        return t // n_tiles, lax.rem(t, n_tiles)

    def prev(t):
        t = jnp.maximum(t - 1, 0)
        return t // n_tiles, lax.rem(t, n_tiles)

    return pl.pallas_call(
        functools.partial(_outproj_kernel, n_tiles, n_steps),
        grid=(n_steps,),
        in_specs=[pl.BlockSpec((tm, ka), lambda t: (cur(t)[0], 0)),
                  pl.BlockSpec((tm, kr), lambda t: (cur(t)[0], 0)),
                  pl.BlockSpec((None, ka, tn), lambda t: (layer, 0, cur(t)[1])),
                  pl.BlockSpec((None, kr, tn), lambda t: (layer, 1, cur(t)[1])),
                  pl.BlockSpec((tm, tn), lambda t: prev(t)),
                  pl.BlockSpec((1, tn), lambda t: (0, prev(t)[1]))],
        out_specs=[pl.BlockSpec((tm, tn), lambda t: prev(t)),
                   pl.BlockSpec((tm, tn), lambda t: prev(t)),
                   pl.BlockSpec((tm, LANES), lambda t: (prev(t)[0], 0))],
        out_shape=[jax.ShapeDtypeStruct((m, d), F32),
                   jax.ShapeDtypeStruct((m, d), BF16),
                   jax.ShapeDtypeStruct((m, LANES), F32)],
        scratch_shapes=[pltpu.VMEM((tm, tn), F32)],
        compiler_params=_cparams("arbitrary"),
        name="outproj",
    )(a, r, w, w, x2, gnext)


FINAL_TM = 1024


def _outproj_final_kernel(n_tiles, n_mm, d_model, a_ref, r_ref, wa_ref, wr_ref, x_ref, g_ref, o_ref,
                          acc_ref, xrow_ref, ssq_ref):
    t = pl.program_id(0)
    fin = jnp.maximum(t - 1, 0)
    fin_row, fin_col = fin // n_tiles, lax.rem(fin, n_tiles)
    emi = jnp.maximum(t - 1 - n_tiles, 0)
    emi_row, emi_col = emi // n_tiles, lax.rem(emi, n_tiles)

    def matmul():
        acc = jnp.dot(a_ref[...], wa_ref[...], preferred_element_type=F32)
        acc_ref[...] = acc + jnp.dot(r_ref[...], wr_ref[...], preferred_element_type=F32)

    def finish():
        half = lax.rem(fin_row, 2)
        xn = x_ref[...] + acc_ref[...]
        xrow_ref[fin_col] = xn
        part = jnp.broadcast_to(jnp.sum(xn * xn, axis=-1, keepdims=True), ssq_ref.shape[1:])
        ssq_ref[half] = ssq_ref[half] + part

    def emit():
        half = lax.rem(emi_row, 2)
        ssq = ssq_ref[half]
        scale = lax.rsqrt(ssq[:, :1] * (1.0 / d_model) + EPS)
        o_ref[...] = xrow_ref[emi_col] * scale * g_ref[...]
        ssq_ref[half] = ssq * jnp.where(emi_col == n_tiles - 1, 0.0, 1.0)

    @pl.when(t == 0)
    def _():
        ssq_ref[...] = jnp.zeros(ssq_ref.shape, F32)
        matmul()

    @pl.when(jnp.logical_and(t >= 1, t <= n_tiles))
    def _():
        finish()
        matmul()

    @pl.when(jnp.logical_and(t > n_tiles, t < n_mm))
    def _():
        emit()
        finish()
        matmul()

    @pl.when(t == n_mm)
    def _():
        emit()
        finish()

    @pl.when(t > n_mm)
    def _():
        emit()


def _outproj_final(a, r, w, layer, x2, g):
    m, d = x2.shape
    ka, kr = a.shape[1], r.shape[1]
    assert ka == kr and w.shape[1] == ka + kr
    tm = min(FINAL_TM, m)
    tn = min(512, d)
    n_tiles = d // tn
    n_mm = (m // tm) * n_tiles
    n_steps = n_mm + n_tiles + 1
    assert n_mm > n_tiles

    def cur(t):
        t = jnp.minimum(t, n_mm - 1)
        return t // n_tiles, lax.rem(t, n_tiles)

    def fin(t):
        t = jnp.clip(t - 1, 0, n_mm - 1)
        return t // n_tiles, lax.rem(t, n_tiles)

    def emi(t):
        t = jnp.maximum(t - 1 - n_tiles, 0)
        return t // n_tiles, lax.rem(t, n_tiles)

    return pl.pallas_call(
        functools.partial(_outproj_final_kernel, n_tiles, n_mm, d),
        grid=(n_steps,),
        in_specs=[pl.BlockSpec((tm, ka), lambda t: (cur(t)[0], 0)),
                  pl.BlockSpec((tm, kr), lambda t: (cur(t)[0], 0)),
                  pl.BlockSpec((None, ka, tn), lambda t: (layer, 0, cur(t)[1])),
                  pl.BlockSpec((None, kr, tn), lambda t: (layer, 1, cur(t)[1])),
                  pl.BlockSpec((tm, tn), lambda t: fin(t)),
                  pl.BlockSpec((1, tn), lambda t: (0, emi(t)[1]))],
        out_specs=pl.BlockSpec((tm, tn), lambda t: emi(t)),
        out_shape=jax.ShapeDtypeStruct((m, d), F32),
        scratch_shapes=[pltpu.VMEM((tm, tn), F32),
                        pltpu.VMEM((n_tiles, tm, tn), F32),
                        pltpu.VMEM((2, tm, LANES), F32)],
        compiler_params=_cparams("arbitrary"),
        name="outproj_final",
    )(a, r, w, w, x2, g)


def _rope_tables(seq):
    rows = seq // GRID_W
    row = jnp.repeat(jnp.arange(rows), GRID_W).astype(F32)
    col = jnp.tile(jnp.arange(GRID_W), rows).astype(F32)
    axis_dim = HEAD_DIM // 2
    inv = ROPE_THETA ** (-jnp.arange(0, axis_dim, 2, dtype=F32) / axis_dim)
    ang_r = row[:, None] * inv[None, :]
    ang_c = col[:, None] * inv[None, :]
    cr, sr, cc, sc = jnp.cos(ang_r), jnp.sin(ang_r), jnp.cos(ang_c), jnp.sin(ang_c)
    z = jnp.zeros_like(sr)
    cos = jnp.concatenate([cr, cr, cc, cc], axis=-1)
    sin_up = jnp.concatenate([-sr, z, -sc, z], axis=-1)
    sin_dn = jnp.concatenate([z, sr, z, sc], axis=-1)
    return cos, sin_up, sin_dn


def kernel(x, norm_w, w_in, q_norm, k_norm, ret_decay_fwd, ret_decay_bwd, ret_norm, w_out, final_norm):
    batch, seq, d = x.shape
    depth = w_in.shape[0]
    attn_width = d // 2
    kv_width = attn_width // ATTN_GROUP
    ret_width = d - attn_width
    ret_heads = ret_width // RET_V_DIM
    ret_qk_width = ret_heads * RET_QK_DIM
    m = batch * seq

    ref_widths = (("aq", attn_width), ("ak", kv_width), ("av", kv_width), ("ag", attn_width),
                  ("rq", ret_qk_width), ("rk", ret_qk_width), ("rv", ret_width), ("rg", ret_width))
    ref_pos, pos = {}, 0
    for name, wdt in ref_widths:
        ref_pos[name] = (pos, wdt)
        pos += wdt
    assert pos == w_in.shape[2]
    tn = min(512, kv_width)
    offs, kind_ranges, gain_starts, pos = {}, [], [], 0
    for name in SECTION_ORDER:
        wdt = ref_pos[name][1]
        assert wdt % tn == 0
        offs[name] = pos
        kind = SECTION_KIND[name]
        if kind_ranges and kind_ranges[-1][0] == kind:
            kind_ranges[-1] = (kind, kind_ranges[-1][1], (pos + wdt) // tn)
        else:
            kind_ranges.append((kind, pos // tn, (pos + wdt) // tn))
        if name in GAINED:
            gain_starts.append(pos // tn)
        pos += wdt

    tile_shifts = tuple((offs[name] // tn, (offs[name] + ref_pos[name][1]) // tn, (ref_pos[name][0] - offs[name]) // tn)
                        for name in SECTION_ORDER)
    w_out_b = w_out.astype(BF16)
    tables = _rope_tables(seq)
    attn_scale = HEAD_DIM ** -0.5
    x2 = x.reshape(m, d)

    xg, ssq = _prologue(x2, norm_w[0].reshape(1, d))
    for l in range(depth):
        gains = jnp.stack([q_norm[l] * (attn_scale * LOG2E),
                           k_norm[l],
                           jnp.ones((HEAD_DIM,), F32),
                           jnp.full((HEAD_DIM,), RET_QK_DIM ** -0.5, F32)]).reshape(len(GAINED), 1, HEAD_DIM)
        proj = _inproj(xg, w_in, l, ssq, tables, gains, seq=seq, kind_ranges=tuple(kind_ranges),
                       gain_starts=tuple(gain_starts), tile_shifts=tile_shifts, tn=tn)
        a = _attention(proj, gains.reshape(len(GAINED), HEAD_DIM), batch=batch, seq=seq, attn_width=attn_width, kv_width=kv_width, offs=offs)
        dfw = jnp.broadcast_to(ret_decay_fwd[l][:, None, None], (ret_heads, 1, LANES))
        dbw = jnp.broadcast_to(ret_decay_bwd[l][:, None, None], (ret_heads, 1, LANES))
        r = _retention(proj, dfw, dbw, ret_norm[l].reshape(ret_heads, 1, RET_V_DIM),
                       batch=batch, seq=seq, offs=offs, ret_heads=ret_heads)
        if l + 1 < depth:
            x2, xg, ssq = _outproj(a, r, w_out_b, l, x2, norm_w[l + 1].reshape(1, d))
        else:
            out = _outproj_final(a, r, w_out_b, l, x2, final_norm.reshape(1, d))
    return out.reshape(batch, seq, d)
```

```python
import functools
import math

import jax
import jax.numpy as jnp
from jax import lax
from jax.experimental import pallas as pl
from jax.experimental.pallas import tpu as pltpu

F32 = jnp.float32
BF16 = jnp.bfloat16

GRID_W = 64
HEAD_DIM = 128
ATTN_GROUP = 4
RET_V_DIM = 256
RET_QK_DIM = 128
ROPE_THETA = 10000.0
EPS = 1e-6
LOG2E = math.log2(math.e)

LANES = 128
VMEM_LIMIT = 56 * 1024 * 1024

NT_DIMS = (((1,), (1,)), ((), ()))


def _cparams(*sem):
    return pltpu.CompilerParams(dimension_semantics=sem, vmem_limit_bytes=VMEM_LIMIT)


def _prologue_kernel(x_ref, g_ref, xg_ref, ssq_ref):
    x = x_ref[...]
    xg_ref[...] = (x * g_ref[...]).astype(BF16)
    ssq_ref[...] = jnp.broadcast_to(jnp.sum(x * x, axis=-1, keepdims=True), ssq_ref.shape)


def _prologue(x2, g):
    m, d = x2.shape
    tm = min(256, m)
    return pl.pallas_call(
        _prologue_kernel,
        grid=(m // tm,),
        in_specs=[pl.BlockSpec((tm, d), lambda i: (i, 0)),
                  pl.BlockSpec((1, d), lambda i: (0, 0))],
        out_specs=[pl.BlockSpec((tm, d), lambda i: (i, 0)),
                   pl.BlockSpec((tm, LANES), lambda i: (i, 0))],
        out_shape=[jax.ShapeDtypeStruct((m, d), BF16),
                   jax.ShapeDtypeStruct((m, LANES), F32)],
        compiler_params=_cparams("parallel"),
        name="prologue",
    )(x2, g)


SECTION_ORDER = ("ag", "rg", "aq", "ak", "rq", "rk", "av", "rv")
SECTION_KIND = {"ag": "gate", "rg": "gate", "aq": "norm_rope", "ak": "norm_rope",
                "rq": "rope", "rk": "rope", "av": "plain", "rv": "plain"}
GAINED = ("aq", "ak", "rq", "rk")
ROW_CHUNK = 512


def _rope(y, cos, sin_up, sin_dn):
    q = HEAD_DIM // 4
    return y * cos + pltpu.roll(y, HEAD_DIM - q, 1) * sin_up + pltpu.roll(y, q, 1) * sin_dn


def _silu(y):
    h = 0.5 * y
    return h + h * jnp.tanh(h)


def _inproj_kernel(kind_ranges, n_tiles, n_steps, d_model, xg_ref, w_ref, ssq_ref, cos_ref, sup_ref, sdn_ref,
                   gain_ref, o_ref, acc_ref):
    t = pl.program_id(0)
    prev_col = lax.rem(t + (n_tiles - 1), n_tiles)
    tm, tn = acc_ref.shape

    def matmul():
        acc_ref[...] = jnp.dot(xg_ref[...], w_ref[...].astype(BF16), preferred_element_type=F32)

    def postprocess(kind):
        for c in range(tm // ROW_CHUNK):
            rows = slice(c * ROW_CHUNK, (c + 1) * ROW_CHUNK)
            r = lax.rsqrt(ssq_ref[rows, :] * (1.0 / d_model) + EPS)
            for g in range(tn // LANES):
                cols = slice(g * LANES, (g + 1) * LANES)
                y = acc_ref[rows, cols] * r
                if kind == "norm_rope":
                    ms = jnp.mean(y * y, axis=-1, keepdims=True)
                    y = _rope(y * lax.rsqrt(ms + EPS) * gain_ref[...],
                              cos_ref[rows, :], sup_ref[rows, :], sdn_ref[rows, :])
                elif kind == "rope":
                    y = _rope(y * gain_ref[...], cos_ref[rows, :], sup_ref[rows, :], sdn_ref[rows, :])
                elif kind == "gate":
                    y = _silu(y)
                o_ref[rows, cols] = y.astype(BF16)

    @pl.when(t == 0)
    def _():
        matmul()

    middle = jnp.logical_and(t > 0, t < n_steps - 1)
    for kind, lo, hi in kind_ranges:
        @pl.when(jnp.logical_and(middle, jnp.logical_and(prev_col >= lo, prev_col < hi)))
        def _(kind=kind):
            postprocess(kind)
            matmul()

    @pl.when(t == n_steps - 1)
    def _():
        postprocess(kind_ranges[-1][0])


def _inproj(xg, w, layer, ssq, tables, gains, *, seq, kind_ranges, gain_starts, tile_shifts, tn):
    m, d = xg.shape
    n = w.shape[2]
    n_tiles = n // tn
    tm = min(1024, seq)
    sb = seq // tm
    n_steps = (m // tm) * n_tiles + 1
    assert kind_ranges[-1][2] == n_tiles and tm % ROW_CHUNK == 0

    def cur(t):
        t = jnp.minimum(t, n_steps - 2)
        return t // n_tiles, lax.rem(t, n_tiles)

    def prev(t):
        t = jnp.maximum(t - 1, 0)
        return t // n_tiles, lax.rem(t, n_tiles)

    def gain_index(t):
        col = prev(t)[1]
        return (sum((col >= s).astype(jnp.int32) for s in gain_starts[1:]), 0, 0)

    def weight_index(t):
        col = cur(t)[1]
        shift = sum(jnp.where(jnp.logical_and(col >= lo, col < hi), sh, 0) for lo, hi, sh in tile_shifts)
        return (layer, 0, col + shift)

    kern = functools.partial(_inproj_kernel, kind_ranges, n_tiles, n_steps, d)
    tab = pl.BlockSpec((tm, LANES), lambda t: (lax.rem(prev(t)[0], sb), 0))
    return pl.pallas_call(
        kern,
        grid=(n_steps,),
        in_specs=[pl.BlockSpec((tm, d), lambda t: (cur(t)[0], 0)),
                  pl.BlockSpec((None, d, tn), weight_index),
                  pl.BlockSpec((tm, LANES), lambda t: (prev(t)[0], 0)),
                  tab, tab, tab,
                  pl.BlockSpec((None, 1, LANES), gain_index)],
        out_specs=pl.BlockSpec((tm, tn), lambda t: prev(t)),
        out_shape=jax.ShapeDtypeStruct((m, n), BF16),
        scratch_shapes=[pltpu.VMEM((tm, tn), F32)],
        compiler_params=_cparams("arbitrary"),
        name="inproj",
    )(xg, w, ssq, *tables, gains)


def _transpose_via_mxu(a):
    rows = lax.broadcasted_iota(jnp.int32, (LANES, LANES), 0)
    cols = lax.broadcasted_iota(jnp.int32, (LANES, LANES), 1)
    eye = jnp.where(rows == cols, 1.0, 0.0).astype(BF16)
    return lax.dot_general(eye, a, NT_DIMS, preferred_element_type=F32).astype(BF16)


SCORE_LIMIT = 50.0
Q_UNROLL = 4


def _attn_kernel(tq, tk, q_ref, k_ref, v_ref, g_ref, gains_ref, w_ref, o_ref, wb_ref, vt_ref):
    wb_ref[...] = w_ref[...].astype(BF16)
    seq = q_ref.shape[0]
    nq = ATTN_GROUP * tq
    vt_ref[...] = _transpose_via_mxu(v_ref[...])
    gain_q = jnp.max(jnp.abs(gains_ref[0:1, :]))
    gain_k = jnp.max(jnp.abs(gains_ref[1:2, :]))
    bound = gain_q * gain_k * (HEAD_DIM * 1.02)

    def load_q(qi):
        rows = pl.ds(pl.multiple_of(qi * tq, tq), tq)
        qb = q_ref[rows, :]
        qs = jnp.concatenate([qb[:, g * LANES:(g + 1) * LANES] for g in range(ATTN_GROUP)], axis=0)
        return rows, qs

    def finish(rows, o):
        for g in range(ATTN_GROUP):
            cols = slice(g * LANES, (g + 1) * LANES)
            og = o[:, g * tq:(g + 1) * tq].T
            o_ref[rows, cols] = (g_ref[rows, cols].astype(F32) * og).astype(BF16)

    def unshifted():
        def qblock(qi, carry):
            rows, qs = load_q(qi)

            def kvstep(c, st):
                l, acc = st
                c0 = pl.multiple_of(c * tk, tk)
                s = lax.dot_general(k_ref[pl.ds(c0, tk), :], qs, NT_DIMS,
                                    preferred_element_type=F32)
                p = jnp.exp2(s)
                l = l + jnp.sum(p.reshape(tk // 8, 8, nq), axis=0)
                return l, acc + jnp.dot(vt_ref[:, pl.ds(c0, tk)], p.astype(BF16), preferred_element_type=F32)

            init = (jnp.zeros((8, nq), F32), jnp.zeros((HEAD_DIM, nq), F32))
            l, acc = lax.fori_loop(0, seq // tk, kvstep, init, unroll=True)
            finish(rows, acc * (1.0 / jnp.sum(l, axis=0, keepdims=True)))
            return carry

        lax.fori_loop(0, seq // tq, qblock, 0, unroll=Q_UNROLL)

    def running_max():
        def qblock(qi, carry):
            rows, qs = load_q(qi)

            def kvstep(c, st):
                m, l, acc = st
                c0 = pl.multiple_of(c * tk, tk)
                s = lax.dot_general(k_ref[pl.ds(c0, tk), :], qs, NT_DIMS,
                                    preferred_element_type=F32)
                m_new = jnp.maximum(m, jnp.max(s, axis=0, keepdims=True))
                alpha = jnp.exp2(m - m_new)
                p = jnp.exp2(s - m_new)
                l = alpha * l + jnp.sum(p, axis=0, keepdims=True)
                pv = jnp.dot(vt_ref[:, pl.ds(c0, tk)], p.astype(BF16), preferred_element_type=F32)
                return m_new, l, alpha * acc + pv

            init = (jnp.full((1, nq), -1e30, F32), jnp.zeros((1, nq), F32), jnp.zeros((HEAD_DIM, nq), F32))
            _, l, acc = lax.fori_loop(0, seq // tk, kvstep, init)
            finish(rows, acc * (1.0 / l))
            return carry

        lax.fori_loop(0, seq // tq, qblock, 0)

    lax.cond(bound <= SCORE_LIMIT, unshifted, running_max)


def _attention(proj, gains, w_out, layer, *, batch, seq, attn_width, kv_width, offs):
    kvh = kv_width // HEAD_DIM
    wrows, wcols = w_out.shape[1:]
    slab = wrows // (batch * kvh)
    assert slab * batch * kvh == wrows and slab % 16 == 0
    gw = ATTN_GROUP * HEAD_DIM
    tq = min(256, seq)
    tk = min(1024, seq)
    assert offs["aq"] % gw == 0 and offs["ag"] % gw == 0
    q_off = offs["aq"] // gw
    k_off = offs["ak"] // HEAD_DIM
    v_off = offs["av"] // HEAD_DIM
    g_off = offs["ag"] // gw
    return pl.pallas_call(
        functools.partial(_attn_kernel, tq, tk),
        grid=(batch, kvh),
        in_specs=[pl.BlockSpec((seq, gw), lambda b, h: (b, q_off + h)),
                  pl.BlockSpec((seq, HEAD_DIM), lambda b, h: (b, k_off + h)),
                  pl.BlockSpec((seq, HEAD_DIM), lambda b, h: (b, v_off + h)),
                  pl.BlockSpec((seq, gw), lambda b, h: (b, g_off + h)),
                  pl.BlockSpec(gains.shape, lambda b, h: (0, 0)),
                  pl.BlockSpec((None, slab, wcols), lambda b, h: (layer, b * kvh + h, 0))],
        out_specs=[pl.BlockSpec((seq, gw), lambda b, h: (b, h)),
                   pl.BlockSpec((slab, wcols), lambda b, h: (b * kvh + h, 0))],
        out_shape=[jax.ShapeDtypeStruct((batch * seq, attn_width), BF16),
                   jax.ShapeDtypeStruct((wrows, wcols), BF16)],
        scratch_shapes=[pltpu.VMEM((HEAD_DIM, seq), BF16)],
        compiler_params=_cparams("parallel", "parallel"),
        name="attention",
    )(proj, proj, proj, proj, gains, w_out)


RET_UNROLL = 16
RET_HEADS_PER_STEP = 2


def _log_sigmoid(x):
    return jnp.minimum(x, 0.0) - jnp.log1p(jnp.exp(-jnp.abs(x)))


def _retention_kernel(ck, q_ref, k_ref, v_ref, g_ref, df_ref, db_ref, gn_ref, o_ref, kt_ref, sb_ref):
    seq = q_ref.shape[0]
    nchunk = seq // ck
    heads = range(RET_HEADS_PER_STEP)
    qk = lambda e: slice(e * RET_QK_DIM, (e + 1) * RET_QK_DIM)
    vv = lambda e: slice(e * RET_V_DIM, (e + 1) * RET_V_DIM)

    diff = (lax.broadcasted_iota(jnp.int32, (ck, ck), 0)
            - lax.broadcasted_iota(jnp.int32, (ck, ck), 1)).astype(F32)
    rowi = lax.broadcasted_iota(jnp.int32, (ck, RET_QK_DIM), 0).astype(F32)
    coli = lax.broadcasted_iota(jnp.int32, (RET_QK_DIM, ck), 1).astype(F32)
    decay, xi_f, xi_b, zt_f, zt_b, gc_f, gc_b = [], [], [], [], [], [], []
    for e in heads:
        lgf = _log_sigmoid(df_ref[e])[:, :1]
        lgb = _log_sigmoid(db_ref[e])[:, :1]
        kt_ref[e] = _transpose_via_mxu(k_ref[:, qk(e)])
        decay.append(jnp.exp(jnp.where(diff >= 0, lgf * diff, -lgb * diff)))
        xi_f.append(jnp.exp(lgf * (rowi + 1.0)))
        xi_b.append(jnp.exp(lgb * (ck - rowi)))
        zt_f.append(jnp.exp(lgf * (ck - 1.0 - coli)))
        zt_b.append(jnp.exp(lgb * coli))
        gc_f.append(jnp.exp(lgf * ck))
        gc_b.append(jnp.exp(lgb * ck))

    def kz(e, c0, zt):
        return (kt_ref[e, :, pl.ds(c0, ck)].astype(F32) * zt).astype(BF16)

    def bstep(t, states):
        jc = nchunk - 1 - t
        c0 = pl.multiple_of(jc * ck, ck)
        new = []
        for e in heads:
            sb_ref[e, jc] = states[e].astype(BF16)
            upd = jnp.dot(kz(e, c0, zt_b[e]), v_ref[pl.ds(c0, ck), vv(e)], preferred_element_type=F32)
            new.append(gc_b[e] * states[e] + upd)
        return tuple(new)

    zeros = tuple(jnp.zeros((RET_QK_DIM, RET_V_DIM), F32) for _ in heads)
    lax.fori_loop(0, nchunk, bstep, zeros, unroll=RET_UNROLL)

    def fstep(jc, states):
        c0 = pl.multiple_of(jc * ck, ck)
        rows = pl.ds(c0, ck)
        new = []
        for e in heads:
            q = q_ref[rows, qk(e)]
            v = v_ref[rows, vv(e)]
            s = jnp.dot(q, kt_ref[e, :, pl.ds(c0, ck)], preferred_element_type=F32)
            out = jnp.dot((s * decay[e]).astype(BF16), v, preferred_element_type=F32)
            qf = q.astype(F32)
            qx = jnp.concatenate([(qf * xi_f[e]).astype(BF16), (qf * xi_b[e]).astype(BF16)], axis=1)
            both = jnp.concatenate([states[e].astype(BF16), sb_ref[e, jc]], axis=0)
            out = out + jnp.dot(qx, both, preferred_element_type=F32)
            ms = jnp.mean(out * out, axis=-1, keepdims=True)
            y = out * lax.rsqrt(ms + EPS) * gn_ref[e]
            o_ref[rows, vv(e)] = (g_ref[rows, vv(e)].astype(F32) * y).astype(BF16)
            upd = jnp.dot(kz(e, c0, zt_f[e]), v, preferred_element_type=F32)
            new.append(gc_f[e] * states[e] + upd)
        return tuple(new)

    lax.fori_loop(0, nchunk, fstep, zeros, unroll=RET_UNROLL)


def _retention(proj, dfw, dbw, gnorm, *, batch, seq, offs, ret_heads):
    hps = RET_HEADS_PER_STEP
    wq, wv = hps * RET_QK_DIM, hps * RET_V_DIM
    off_q, off_k, off_v, off_g = offs["rq"], offs["rk"], offs["rv"], offs["rg"]
    assert ret_heads % hps == 0 and off_q % wq == 0 and off_k % wq == 0 and off_v % wv == 0 and off_g % wv == 0
    ck = min(256, seq)
    vec = lambda w: pl.BlockSpec((hps, 1, w), lambda b, h: (h, 0, 0))
    return pl.pallas_call(
        functools.partial(_retention_kernel, ck),
        grid=(batch, ret_heads // hps),
        in_specs=[pl.BlockSpec((seq, wq), lambda b, h: (b, off_q // wq + h)),
                  pl.BlockSpec((seq, wq), lambda b, h: (b, off_k // wq + h)),
                  pl.BlockSpec((seq, wv), lambda b, h: (b, off_v // wv + h)),
                  pl.BlockSpec((seq, wv), lambda b, h: (b, off_g // wv + h)),
                  vec(LANES), vec(LANES), vec(RET_V_DIM)],
        out_specs=pl.BlockSpec((seq, wv), lambda b, h: (b, h)),
        out_shape=jax.ShapeDtypeStruct((batch * seq, ret_heads * RET_V_DIM), BF16),
        scratch_shapes=[pltpu.VMEM((hps, RET_QK_DIM, seq), BF16),
                        pltpu.VMEM((hps, seq // ck, RET_QK_DIM, RET_V_DIM), BF16)],
        compiler_params=_cparams("parallel", "parallel"),
        name="retention",
    )(proj, proj, proj, proj, dfw, dbw, gnorm)


def _outproj_kernel(n_tiles, n_steps, a_ref, r_ref, wa_ref, wr_ref, x_ref, g_ref, xo_ref, xg_ref, ssq_ref, acc_ref):
    t = pl.program_id(0)
    prev_col = lax.rem(t + (n_tiles - 1), n_tiles)

    def matmul():
        acc = jnp.dot(a_ref[...], wa_ref[...], preferred_element_type=F32)
        acc_ref[...] = acc + jnp.dot(r_ref[...], wr_ref[...], preferred_element_type=F32)

    def finish():
        xn = x_ref[...] + acc_ref[...]
        xo_ref[...] = xn
        xg_ref[...] = (xn * g_ref[...]).astype(BF16)
        return jnp.broadcast_to(jnp.sum(xn * xn, axis=-1, keepdims=True), ssq_ref.shape)

    def accumulate(part):
        @pl.when(prev_col == 0)
        def _():
            ssq_ref[...] = part

        @pl.when(prev_col > 0)
        def _():
            ssq_ref[...] = ssq_ref[...] + part

    @pl.when(t == 0)
    def _():
        matmul()

    @pl.when(jnp.logical_and(t > 0, t < n_steps - 1))
    def _():
        part = finish()
        matmul()
        accumulate(part)

    @pl.when(t == n_steps - 1)
    def _():
        accumulate(finish())


def _outproj(a, r, w, x2, gnext):
    m, d = x2.shape
    ka, kr = a.shape[1], r.shape[1]
    assert ka == kr and w.shape[0] == ka + kr
    tm = min(1024, m)
    tn = min(512, d)
    n_tiles = d // tn
    n_steps = (m // tm) * n_tiles + 1

    def cur(t):
        t = jnp.minimum(t, n_steps - 2)
        return t // n_tiles, lax.rem(t, n_tiles)

    def prev(t):
        t = jnp.maximum(t - 1, 0)
        return t // n_tiles, lax.rem(t, n_tiles)

    return pl.pallas_call(
        functools.partial(_outproj_kernel, n_tiles, n_steps),
        grid=(n_steps,),
        in_specs=[pl.BlockSpec((tm, ka), lambda t: (cur(t)[0], 0)),
                  pl.BlockSpec((tm, kr), lambda t: (cur(t)[0], 0)),
                  pl.BlockSpec((ka, tn), lambda t: (0, cur(t)[1])),
                  pl.BlockSpec((kr, tn), lambda t: (1, cur(t)[1])),
                  pl.BlockSpec((tm, tn), lambda t: prev(t)),
                  pl.BlockSpec((1, tn), lambda t: (0, prev(t)[1]))],
        out_specs=[pl.BlockSpec((tm, tn), lambda t: prev(t)),
                   pl.BlockSpec((tm, tn), lambda t: prev(t)),
                   pl.BlockSpec((tm, LANES), lambda t: (prev(t)[0], 0))],
        out_shape=[jax.ShapeDtypeStruct((m, d), F32),
                   jax.ShapeDtypeStruct((m, d), BF16),
                   jax.ShapeDtypeStruct((m, LANES), F32)],
        scratch_shapes=[pltpu.VMEM((tm, tn), F32)],
        compiler_params=_cparams("arbitrary"),
        name="outproj",
    )(a, r, w, w, x2, gnext)


FINAL_TM = 1024


def _outproj_final_kernel(n_tiles, n_mm, d_model, a_ref, r_ref, wa_ref, wr_ref, x_ref, g_ref, o_ref,
                          acc_ref, xrow_ref, ssq_ref):
    t = pl.program_id(0)
    fin = jnp.maximum(t - 1, 0)
    fin_row, fin_col = fin // n_tiles, lax.rem(fin, n_tiles)
    emi = jnp.maximum(t - 1 - n_tiles, 0)
    emi_row, emi_col = emi // n_tiles, lax.rem(emi, n_tiles)

    def matmul():
        acc = jnp.dot(a_ref[...], wa_ref[...], preferred_element_type=F32)
        acc_ref[...] = acc + jnp.dot(r_ref[...], wr_ref[...], preferred_element_type=F32)

    def finish():
        half = lax.rem(fin_row, 2)
        xn = x_ref[...] + acc_ref[...]
        xrow_ref[fin_col] = xn
        part = jnp.broadcast_to(jnp.sum(xn * xn, axis=-1, keepdims=True), ssq_ref.shape[1:])
        ssq_ref[half] = ssq_ref[half] + part

    def emit():
        half = lax.rem(emi_row, 2)
        ssq = ssq_ref[half]
        scale = lax.rsqrt(ssq[:, :1] * (1.0 / d_model) + EPS)
        o_ref[...] = xrow_ref[emi_col] * scale * g_ref[...]
        ssq_ref[half] = ssq * jnp.where(emi_col == n_tiles - 1, 0.0, 1.0)

    @pl.when(t == 0)
    def _():
        ssq_ref[...] = jnp.zeros(ssq_ref.shape, F32)
        matmul()

    @pl.when(jnp.logical_and(t >= 1, t <= n_tiles))
    def _():
        finish()
        matmul()

    @pl.when(jnp.logical_and(t > n_tiles, t < n_mm))
    def _():
        emit()
        finish()
        matmul()

    @pl.when(t == n_mm)
    def _():
        emit()
        finish()

    @pl.when(t > n_mm)
    def _():
        emit()


def _outproj_final(a, r, w, x2, g):
    m, d = x2.shape
    ka, kr = a.shape[1], r.shape[1]
    assert ka == kr and w.shape[0] == ka + kr
    tm = min(FINAL_TM, m)
    tn = min(512, d)
    n_tiles = d // tn
    n_mm = (m // tm) * n_tiles
    n_steps = n_mm + n_tiles + 1
    assert n_mm > n_tiles

    def cur(t):
        t = jnp.minimum(t, n_mm - 1)
        return t // n_tiles, lax.rem(t, n_tiles)

    def fin(t):
        t = jnp.clip(t - 1, 0, n_mm - 1)
        return t // n_tiles, lax.rem(t, n_tiles)

    def emi(t):
        t = jnp.maximum(t - 1 - n_tiles, 0)
        return t // n_tiles, lax.rem(t, n_tiles)

    return pl.pallas_call(
        functools.partial(_outproj_final_kernel, n_tiles, n_mm, d),
        grid=(n_steps,),
        in_specs=[pl.BlockSpec((tm, ka), lambda t: (cur(t)[0], 0)),
                  pl.BlockSpec((tm, kr), lambda t: (cur(t)[0], 0)),
                  pl.BlockSpec((ka, tn), lambda t: (0, cur(t)[1])),
                  pl.BlockSpec((kr, tn), lambda t: (1, cur(t)[1])),
                  pl.BlockSpec((tm, tn), lambda t: fin(t)),
                  pl.BlockSpec((1, tn), lambda t: (0, emi(t)[1]))],
        out_specs=pl.BlockSpec((tm, tn), lambda t: emi(t)),
        out_shape=jax.ShapeDtypeStruct((m, d), F32),
        scratch_shapes=[pltpu.VMEM((tm, tn), F32),
                        pltpu.VMEM((n_tiles, tm, tn), F32),
                        pltpu.VMEM((2, tm, LANES), F32)],
        compiler_params=_cparams("arbitrary"),
        name="outproj_final",
    )(a, r, w, w, x2, g)


def _rope_tables(seq):
    rows = seq // GRID_W
    row = jnp.repeat(jnp.arange(rows), GRID_W).astype(F32)
    col = jnp.tile(jnp.arange(GRID_W), rows).astype(F32)
    axis_dim = HEAD_DIM // 2
    inv = ROPE_THETA ** (-jnp.arange(0, axis_dim, 2, dtype=F32) / axis_dim)
    ang_r = row[:, None] * inv[None, :]
    ang_c = col[:, None] * inv[None, :]
    cr, sr, cc, sc = jnp.cos(ang_r), jnp.sin(ang_r), jnp.cos(ang_c), jnp.sin(ang_c)
    z = jnp.zeros_like(sr)
    cos = jnp.concatenate([cr, cr, cc, cc], axis=-1)
    sin_up = jnp.concatenate([-sr, z, -sc, z], axis=-1)
    sin_dn = jnp.concatenate([z, sr, z, sc], axis=-1)
    return cos, sin_up, sin_dn


def kernel(x, norm_w, w_in, q_norm, k_norm, ret_decay_fwd, ret_decay_bwd, ret_norm, w_out, final_norm):
    batch, seq, d = x.shape
    depth = w_in.shape[0]
    attn_width = d // 2
    kv_width = attn_width // ATTN_GROUP
    ret_width = d - attn_width
    ret_heads = ret_width // RET_V_DIM
    ret_qk_width = ret_heads * RET_QK_DIM
    m = batch * seq

    ref_widths = (("aq", attn_width), ("ak", kv_width), ("av", kv_width), ("ag", attn_width),
                  ("rq", ret_qk_width), ("rk", ret_qk_width), ("rv", ret_width), ("rg", ret_width))
    ref_pos, pos = {}, 0
    for name, wdt in ref_widths:
        ref_pos[name] = (pos, wdt)
        pos += wdt
    assert pos == w_in.shape[2]
    tn = min(512, kv_width)
    offs, kind_ranges, gain_starts, pos = {}, [], [], 0
    for name in SECTION_ORDER:
        wdt = ref_pos[name][1]
        assert wdt % tn == 0
        offs[name] = pos
        kind = SECTION_KIND[name]
        if kind_ranges and kind_ranges[-1][0] == kind:
            kind_ranges[-1] = (kind, kind_ranges[-1][1], (pos + wdt) // tn)
        else:
            kind_ranges.append((kind, pos // tn, (pos + wdt) // tn))
        if name in GAINED:
            gain_starts.append(pos // tn)
        pos += wdt

    tile_shifts = tuple((offs[name] // tn, (offs[name] + ref_pos[name][1]) // tn, (ref_pos[name][0] - offs[name]) // tn)
                        for name in SECTION_ORDER)
    tables = _rope_tables(seq)
    attn_scale = HEAD_DIM ** -0.5
    x2 = x.reshape(m, d)

    xg, ssq = _prologue(x2, norm_w[0].reshape(1, d))
    for l in range(depth):
        gains = jnp.stack([q_norm[l] * (attn_scale * LOG2E),
                           k_norm[l],
                           jnp.ones((HEAD_DIM,), F32),
                           jnp.full((HEAD_DIM,), RET_QK_DIM ** -0.5, F32)]).reshape(len(GAINED), 1, HEAD_DIM)
        proj = _inproj(xg, w_in, l, ssq, tables, gains, seq=seq, kind_ranges=tuple(kind_ranges),
                       gain_starts=tuple(gain_starts), tile_shifts=tile_shifts, tn=tn)
        a, w_out_b = _attention(proj, gains.reshape(len(GAINED), HEAD_DIM), w_out, l, batch=batch, seq=seq,
                                attn_width=attn_width, kv_width=kv_width, offs=offs)
        dfw = jnp.broadcast_to(ret_decay_fwd[l][:, None, None], (ret_heads, 1, LANES))
        dbw = jnp.broadcast_to(ret_decay_bwd[l][:, None, None], (ret_heads, 1, LANES))
        r = _retention(proj, dfw, dbw, ret_norm[l].reshape(ret_heads, 1, RET_V_DIM),
                       batch=batch, seq=seq, offs=offs, ret_heads=ret_heads)
        if l + 1 < depth:
            x2, xg, ssq = _outproj(a, r, w_out_b, x2, norm_w[l + 1].reshape(1, d))
        else:
            out = _outproj_final(a, r, w_out_b, x2, final_norm.reshape(1, d))
    return out.reshape(batch, seq, d)
```

```python
import functools
import math

import jax
import jax.numpy as jnp
from jax import lax
from jax.experimental import pallas as pl
from jax.experimental.pallas import tpu as pltpu

F32 = jnp.float32
BF16 = jnp.bfloat16

GRID_W = 64
HEAD_DIM = 128
ATTN_GROUP = 4
RET_V_DIM = 256
RET_QK_DIM = 128
ROPE_THETA = 10000.0
EPS = 1e-6
LOG2E = math.log2(math.e)

LANES = 128
BF16_ROWS = 16
VMEM_LIMIT = 56 * 1024 * 1024

PROJ_TM = 1024
PROJ_TN = 512
ATTN_TQ = 256
ATTN_TK = 1024
RET_CHUNK = 256
ROWWISE_TM = 256

NT_DIMS = (((1,), (1,)), ((), ()))


def _cparams(*sem):
    return pltpu.CompilerParams(dimension_semantics=sem, vmem_limit_bytes=VMEM_LIMIT)


def _prologue_kernel(x_ref, g_ref, xg_ref, ssq_ref):
    x = x_ref[...]
    xg_ref[...] = (x * g_ref[...]).astype(BF16)
    ssq_ref[...] = jnp.broadcast_to(jnp.sum(x * x, axis=-1, keepdims=True), ssq_ref.shape)


def _prologue(x2, g):
    m, d = x2.shape
    tm = min(ROWWISE_TM, m)
    return pl.pallas_call(
        _prologue_kernel,
        grid=(m // tm,),
        in_specs=[pl.BlockSpec((tm, d), lambda i: (i, 0)),
                  pl.BlockSpec((1, d), lambda i: (0, 0))],
        out_specs=[pl.BlockSpec((tm, d), lambda i: (i, 0)),
                   pl.BlockSpec((tm, LANES), lambda i: (i, 0))],
        out_shape=[jax.ShapeDtypeStruct((m, d), BF16),
                   jax.ShapeDtypeStruct((m, LANES), F32)],
        compiler_params=_cparams("parallel"),
        name="prologue",
    )(x2, g)


SECTION_ORDER = ("ag", "rg", "aq", "ak", "rq", "rk", "av", "rv")
SECTION_KIND = {"ag": "gate", "rg": "gate", "aq": "norm_rope", "ak": "norm_rope",
                "rq": "rope", "rk": "rope", "av": "plain", "rv": "plain"}
GAINED = ("aq", "ak", "rq", "rk")
ROW_CHUNK = 512


def _rope(y, cos, sin_up, sin_dn):
    q = HEAD_DIM // 4
    return y * cos + pltpu.roll(y, HEAD_DIM - q, 1) * sin_up + pltpu.roll(y, q, 1) * sin_dn


def _silu(y):
    h = 0.5 * y
    return h + h * jnp.tanh(h)


def _inproj_kernel(kind_ranges, n_tiles, n_steps, d_model, xg_ref, w_ref, ssq_ref, cos_ref, sup_ref, sdn_ref,
                   gain_ref, o_ref, acc_ref):
    t = pl.program_id(0)
    prev_col = lax.rem(t + (n_tiles - 1), n_tiles)
    tm, tn = acc_ref.shape

    def matmul():
        acc_ref[...] = jnp.dot(xg_ref[...], w_ref[...].astype(BF16), preferred_element_type=F32)

    def postprocess(kind):
        for c in range(tm // ROW_CHUNK):
            rows = slice(c * ROW_CHUNK, (c + 1) * ROW_CHUNK)
            r = lax.rsqrt(ssq_ref[rows, :] * (1.0 / d_model) + EPS)
            for g in range(tn // LANES):
                cols = slice(g * LANES, (g + 1) * LANES)
                y = acc_ref[rows, cols] * r
                if kind == "norm_rope":
                    ms = jnp.mean(y * y, axis=-1, keepdims=True)
                    y = _rope(y * lax.rsqrt(ms + EPS) * gain_ref[...],
                              cos_ref[rows, :], sup_ref[rows, :], sdn_ref[rows, :])
                elif kind == "rope":
                    y = _rope(y * gain_ref[...], cos_ref[rows, :], sup_ref[rows, :], sdn_ref[rows, :])
                elif kind == "gate":
                    y = _silu(y)
                o_ref[rows, cols] = y.astype(BF16)

    @pl.when(t == 0)
    def _():
        matmul()

    middle = jnp.logical_and(t > 0, t < n_steps - 1)
    for kind, lo, hi in kind_ranges:
        @pl.when(jnp.logical_and(middle, jnp.logical_and(prev_col >= lo, prev_col < hi)))
        def _(kind=kind):
            postprocess(kind)
            matmul()

    @pl.when(t == n_steps - 1)
    def _():
        postprocess(kind_ranges[-1][0])


def _inproj(xg, w, layer, ssq, tables, gains, *, seq, kind_ranges, gain_starts, tile_shifts, tn):
    m, d = xg.shape
    n = w.shape[2]
    n_tiles = n // tn
    tm = min(PROJ_TM, seq)
    sb = seq // tm
    n_steps = (m // tm) * n_tiles + 1
    assert kind_ranges[-1][2] == n_tiles and tm % ROW_CHUNK == 0

    def cur(t):
        t = jnp.minimum(t, n_steps - 2)
        return t // n_tiles, lax.rem(t, n_tiles)

    def prev(t):
        t = jnp.maximum(t - 1, 0)
        return t // n_tiles, lax.rem(t, n_tiles)

    def gain_index(t):
        col = prev(t)[1]
        return (sum((col >= s).astype(jnp.int32) for s in gain_starts[1:]), 0, 0)

    def weight_index(t):
        col = cur(t)[1]
        shift = sum(jnp.where(jnp.logical_and(col >= lo, col < hi), sh, 0) for lo, hi, sh in tile_shifts)
        return (layer, 0, col + shift)

    kern = functools.partial(_inproj_kernel, kind_ranges, n_tiles, n_steps, d)
    tab = pl.BlockSpec((tm, LANES), lambda t: (lax.rem(prev(t)[0], sb), 0))
    return pl.pallas_call(
        kern,
        grid=(n_steps,),
        in_specs=[pl.BlockSpec((tm, d), lambda t: (cur(t)[0], 0)),
                  pl.BlockSpec((None, d, tn), weight_index),
                  pl.BlockSpec((tm, LANES), lambda t: (prev(t)[0], 0)),
                  tab, tab, tab,
                  pl.BlockSpec((None, 1, LANES), gain_index)],
        out_specs=pl.BlockSpec((tm, tn), lambda t: prev(t)),
        out_shape=jax.ShapeDtypeStruct((m, n), BF16),
        scratch_shapes=[pltpu.VMEM((tm, tn), F32)],
        compiler_params=_cparams("arbitrary"),
        name="inproj",
    )(xg, w, ssq, *tables, gains)


def _transpose_via_mxu(a):
    rows = lax.broadcasted_iota(jnp.int32, (LANES, LANES), 0)
    cols = lax.broadcasted_iota(jnp.int32, (LANES, LANES), 1)
    eye = jnp.where(rows == cols, 1.0, 0.0).astype(BF16)
    return lax.dot_general(eye, a, NT_DIMS, preferred_element_type=F32).astype(BF16)


SCORE_LIMIT = 50.0
Q_UNROLL = 4


def _attn_kernel(tq, tk, q_ref, k_ref, v_ref, g_ref, gains_ref, w_ref, o_ref, wb_ref, vt_ref):
    wb_ref[...] = w_ref[...].astype(BF16)
    seq = q_ref.shape[0]
    nq = ATTN_GROUP * tq
    vt_ref[...] = _transpose_via_mxu(v_ref[...])
    gain_q = jnp.max(jnp.abs(gains_ref[0:1, :]))
    gain_k = jnp.max(jnp.abs(gains_ref[1:2, :]))
    bound = gain_q * gain_k * (HEAD_DIM * 1.02)

    def load_q(qi):
        rows = pl.ds(pl.multiple_of(qi * tq, tq), tq)
        qb = q_ref[rows, :]
        qs = jnp.concatenate([qb[:, g * LANES:(g + 1) * LANES] for g in range(ATTN_GROUP)], axis=0)
        return rows, qs

    def finish(rows, o):
        for g in range(ATTN_GROUP):
            cols = slice(g * LANES, (g + 1) * LANES)
            og = o[:, g * tq:(g + 1) * tq].T
            o_ref[rows, cols] = (g_ref[rows, cols].astype(F32) * og).astype(BF16)

    def unshifted():
        def qblock(qi, carry):
            rows, qs = load_q(qi)

            def kvstep(c, st):
                l, acc = st
                c0 = pl.multiple_of(c * tk, tk)
                s = lax.dot_general(k_ref[pl.ds(c0, tk), :], qs, NT_DIMS,
                                    preferred_element_type=F32)
                p = jnp.exp2(s)
                l = l + jnp.sum(p.reshape(tk // 8, 8, nq), axis=0)
                return l, acc + jnp.dot(vt_ref[:, pl.ds(c0, tk)], p.astype(BF16), preferred_element_type=F32)

            init = (jnp.zeros((8, nq), F32), jnp.zeros((HEAD_DIM, nq), F32))
            l, acc = lax.fori_loop(0, seq // tk, kvstep, init, unroll=True)
            finish(rows, acc * (1.0 / jnp.sum(l, axis=0, keepdims=True)))
            return carry

        lax.fori_loop(0, seq // tq, qblock, 0, unroll=Q_UNROLL)

    def running_max():
        def qblock(qi, carry):
            rows, qs = load_q(qi)

            def kvstep(c, st):
                m, l, acc = st
                c0 = pl.multiple_of(c * tk, tk)
                s = lax.dot_general(k_ref[pl.ds(c0, tk), :], qs, NT_DIMS,
                                    preferred_element_type=F32)
                m_new = jnp.maximum(m, jnp.max(s, axis=0, keepdims=True))
                alpha = jnp.exp2(m - m_new)
                p = jnp.exp2(s - m_new)
                l = alpha * l + jnp.sum(p, axis=0, keepdims=True)
                pv = jnp.dot(vt_ref[:, pl.ds(c0, tk)], p.astype(BF16), preferred_element_type=F32)
                return m_new, l, alpha * acc + pv

            init = (jnp.full((1, nq), -1e30, F32), jnp.zeros((1, nq), F32), jnp.zeros((HEAD_DIM, nq), F32))
            _, l, acc = lax.fori_loop(0, seq // tk, kvstep, init)
            finish(rows, acc * (1.0 / l))
            return carry

        lax.fori_loop(0, seq // tq, qblock, 0)

    lax.cond(bound <= SCORE_LIMIT, unshifted, running_max)


def _attention(proj, gains, w_out, layer, *, batch, seq, attn_width, kv_width, offs):
    kvh = kv_width // HEAD_DIM
    wrows, wcols = w_out.shape[1:]
    slab = wrows // (batch * kvh)
    assert slab * batch * kvh == wrows and slab % BF16_ROWS == 0
    gw = ATTN_GROUP * HEAD_DIM
    tq = min(ATTN_TQ, seq)
    tk = min(ATTN_TK, seq)
    assert offs["aq"] % gw == 0 and offs["ag"] % gw == 0
    q_off = offs["aq"] // gw
    k_off = offs["ak"] // HEAD_DIM
    v_off = offs["av"] // HEAD_DIM
    g_off = offs["ag"] // gw
    return pl.pallas_call(
        functools.partial(_attn_kernel, tq, tk),
        grid=(batch, kvh),
        in_specs=[pl.BlockSpec((seq, gw), lambda b, h: (b, q_off + h)),
                  pl.BlockSpec((seq, HEAD_DIM), lambda b, h: (b, k_off + h)),
                  pl.BlockSpec((seq, HEAD_DIM), lambda b, h: (b, v_off + h)),
                  pl.BlockSpec((seq, gw), lambda b, h: (b, g_off + h)),
                  pl.BlockSpec(gains.shape, lambda b, h: (0, 0)),
                  pl.BlockSpec((None, slab, wcols), lambda b, h: (layer, b * kvh + h, 0))],
        out_specs=[pl.BlockSpec((seq, gw), lambda b, h: (b, h)),
                   pl.BlockSpec((slab, wcols), lambda b, h: (b * kvh + h, 0))],
        out_shape=[jax.ShapeDtypeStruct((batch * seq, attn_width), BF16),
                   jax.ShapeDtypeStruct((wrows, wcols), BF16)],
        scratch_shapes=[pltpu.VMEM((HEAD_DIM, seq), BF16)],
        compiler_params=_cparams("parallel", "parallel"),
        name="attention",
    )(proj, proj, proj, proj, gains, w_out)


RET_UNROLL = 16
RET_HEADS_PER_STEP = 2


def _log_sigmoid(x):
    return jnp.minimum(x, 0.0) - jnp.log1p(jnp.exp(-jnp.abs(x)))


def _retention_kernel(ck, q_ref, k_ref, v_ref, g_ref, df_ref, db_ref, gn_ref, o_ref, kt_ref, sb_ref):
    seq = q_ref.shape[0]
    nchunk = seq // ck
    heads = range(RET_HEADS_PER_STEP)
    qk = lambda e: slice(e * RET_QK_DIM, (e + 1) * RET_QK_DIM)
    vv = lambda e: slice(e * RET_V_DIM, (e + 1) * RET_V_DIM)

    diff = (lax.broadcasted_iota(jnp.int32, (ck, ck), 0)
            - lax.broadcasted_iota(jnp.int32, (ck, ck), 1)).astype(F32)
    rowi = lax.broadcasted_iota(jnp.int32, (ck, RET_QK_DIM), 0).astype(F32)
    coli = lax.broadcasted_iota(jnp.int32, (RET_QK_DIM, ck), 1).astype(F32)
    decay, xi_f, xi_b, zt_f, zt_b, gc_f, gc_b = [], [], [], [], [], [], []
    for e in heads:
        lgf = _log_sigmoid(df_ref[e])[:, :1]
        lgb = _log_sigmoid(db_ref[e])[:, :1]
        kt_ref[e] = _transpose_via_mxu(k_ref[:, qk(e)])
        decay.append(jnp.exp(jnp.where(diff >= 0, lgf * diff, -lgb * diff)))
        xi_f.append(jnp.exp(lgf * (rowi + 1.0)))
        xi_b.append(jnp.exp(lgb * (ck - rowi)))
        zt_f.append(jnp.exp(lgf * (ck - 1.0 - coli)))
        zt_b.append(jnp.exp(lgb * coli))
        gc_f.append(jnp.exp(lgf * ck))
        gc_b.append(jnp.exp(lgb * ck))

    def kz(e, c0, zt):
        return (kt_ref[e, :, pl.ds(c0, ck)].astype(F32) * zt).astype(BF16)

    def bstep(t, states):
        jc = nchunk - 1 - t
        c0 = pl.multiple_of(jc * ck, ck)
        new = []
        for e in heads:
            sb_ref[e, jc] = states[e].astype(BF16)
            upd = jnp.dot(kz(e, c0, zt_b[e]), v_ref[pl.ds(c0, ck), vv(e)], preferred_element_type=F32)
            new.append(gc_b[e] * states[e] + upd)
        return tuple(new)

    zeros = tuple(jnp.zeros((RET_QK_DIM, RET_V_DIM), F32) for _ in heads)
    lax.fori_loop(0, nchunk, bstep, zeros, unroll=RET_UNROLL)

    def fstep(jc, states):
        c0 = pl.multiple_of(jc * ck, ck)
        rows = pl.ds(c0, ck)
        new = []
        for e in heads:
            q = q_ref[rows, qk(e)]
            v = v_ref[rows, vv(e)]
            s = jnp.dot(q, kt_ref[e, :, pl.ds(c0, ck)], preferred_element_type=F32)
            out = jnp.dot((s * decay[e]).astype(BF16), v, preferred_element_type=F32)
            qf = q.astype(F32)
            qx = jnp.concatenate([(qf * xi_f[e]).astype(BF16), (qf * xi_b[e]).astype(BF16)], axis=1)
            both = jnp.concatenate([states[e].astype(BF16), sb_ref[e, jc]], axis=0)
            out = out + jnp.dot(qx, both, preferred_element_type=F32)
            ms = jnp.mean(out * out, axis=-1, keepdims=True)
            y = out * lax.rsqrt(ms + EPS) * gn_ref[e]
            o_ref[rows, vv(e)] = (g_ref[rows, vv(e)].astype(F32) * y).astype(BF16)
            upd = jnp.dot(kz(e, c0, zt_f[e]), v, preferred_element_type=F32)
            new.append(gc_f[e] * states[e] + upd)
        return tuple(new)

    lax.fori_loop(0, nchunk, fstep, zeros, unroll=RET_UNROLL)


def _retention(proj, dfw, dbw, gnorm, *, batch, seq, offs, ret_heads):
    hps = RET_HEADS_PER_STEP
    wq, wv = hps * RET_QK_DIM, hps * RET_V_DIM
    off_q, off_k, off_v, off_g = offs["rq"], offs["rk"], offs["rv"], offs["rg"]
    assert ret_heads % hps == 0 and off_q % wq == 0 and off_k % wq == 0 and off_v % wv == 0 and off_g % wv == 0
    ck = min(RET_CHUNK, seq)
    vec = lambda w: pl.BlockSpec((hps, 1, w), lambda b, h: (h, 0, 0))
    return pl.pallas_call(
        functools.partial(_retention_kernel, ck),
        grid=(batch, ret_heads // hps),
        in_specs=[pl.BlockSpec((seq, wq), lambda b, h: (b, off_q // wq + h)),
                  pl.BlockSpec((seq, wq), lambda b, h: (b, off_k // wq + h)),
                  pl.BlockSpec((seq, wv), lambda b, h: (b, off_v // wv + h)),
                  pl.BlockSpec((seq, wv), lambda b, h: (b, off_g // wv + h)),
                  vec(LANES), vec(LANES), vec(RET_V_DIM)],
        out_specs=pl.BlockSpec((seq, wv), lambda b, h: (b, h)),
        out_shape=jax.ShapeDtypeStruct((batch * seq, ret_heads * RET_V_DIM), BF16),
        scratch_shapes=[pltpu.VMEM((hps, RET_QK_DIM, seq), BF16),
                        pltpu.VMEM((hps, seq // ck, RET_QK_DIM, RET_V_DIM), BF16)],
        compiler_params=_cparams("parallel", "parallel"),
        name="retention",
    )(proj, proj, proj, proj, dfw, dbw, gnorm)


def _outproj_kernel(n_tiles, n_steps, a_ref, r_ref, wa_ref, wr_ref, x_ref, g_ref, xo_ref, xg_ref, ssq_ref, acc_ref):
    t = pl.program_id(0)
    prev_col = lax.rem(t + (n_tiles - 1), n_tiles)

    def matmul():
        acc = jnp.dot(a_ref[...], wa_ref[...], preferred_element_type=F32)
        acc_ref[...] = acc + jnp.dot(r_ref[...], wr_ref[...], preferred_element_type=F32)

    def finish():
        xn = x_ref[...] + acc_ref[...]
        xo_ref[...] = xn
        xg_ref[...] = (xn * g_ref[...]).astype(BF16)
        return jnp.broadcast_to(jnp.sum(xn * xn, axis=-1, keepdims=True), ssq_ref.shape)

    def accumulate(part):
        @pl.when(prev_col == 0)
        def _():
            ssq_ref[...] = part

        @pl.when(prev_col > 0)
        def _():
            ssq_ref[...] = ssq_ref[...] + part

    @pl.when(t == 0)
    def _():
        matmul()

    @pl.when(jnp.logical_and(t > 0, t < n_steps - 1))
    def _():
        part = finish()
        matmul()
        accumulate(part)

    @pl.when(t == n_steps - 1)
    def _():
        accumulate(finish())


def _outproj(a, r, w, x2, gnext):
    m, d = x2.shape
    ka, kr = a.shape[1], r.shape[1]
    assert ka == kr and w.shape[0] == ka + kr
    tm = min(PROJ_TM, m)
    tn = min(PROJ_TN, d)
    n_tiles = d // tn
    n_steps = (m // tm) * n_tiles + 1

    def cur(t):
        t = jnp.minimum(t, n_steps - 2)
        return t // n_tiles, lax.rem(t, n_tiles)

    def prev(t):
        t = jnp.maximum(t - 1, 0)
        return t // n_tiles, lax.rem(t, n_tiles)

    return pl.pallas_call(
        functools.partial(_outproj_kernel, n_tiles, n_steps),
        grid=(n_steps,),
        in_specs=[pl.BlockSpec((tm, ka), lambda t: (cur(t)[0], 0)),
                  pl.BlockSpec((tm, kr), lambda t: (cur(t)[0], 0)),
                  pl.BlockSpec((ka, tn), lambda t: (0, cur(t)[1])),
                  pl.BlockSpec((kr, tn), lambda t: (1, cur(t)[1])),
                  pl.BlockSpec((tm, tn), lambda t: prev(t)),
                  pl.BlockSpec((1, tn), lambda t: (0, prev(t)[1]))],
        out_specs=[pl.BlockSpec((tm, tn), lambda t: prev(t)),
                   pl.BlockSpec((tm, tn), lambda t: prev(t)),
                   pl.BlockSpec((tm, LANES), lambda t: (prev(t)[0], 0))],
        out_shape=[jax.ShapeDtypeStruct((m, d), F32),
                   jax.ShapeDtypeStruct((m, d), BF16),
                   jax.ShapeDtypeStruct((m, LANES), F32)],
        scratch_shapes=[pltpu.VMEM((tm, tn), F32)],
        compiler_params=_cparams("arbitrary"),
        name="outproj",
    )(a, r, w, w, x2, gnext)


def _outproj_final_kernel(n_tiles, n_mm, d_model, a_ref, r_ref, wa_ref, wr_ref, x_ref, g_ref, o_ref,
                          acc_ref, xrow_ref, ssq_ref):
    t = pl.program_id(0)
    fin = jnp.maximum(t - 1, 0)
    fin_row, fin_col = fin // n_tiles, lax.rem(fin, n_tiles)
    emi = jnp.maximum(t - 1 - n_tiles, 0)
    emi_row, emi_col = emi // n_tiles, lax.rem(emi, n_tiles)

    def matmul():
        acc = jnp.dot(a_ref[...], wa_ref[...], preferred_element_type=F32)
        acc_ref[...] = acc + jnp.dot(r_ref[...], wr_ref[...], preferred_element_type=F32)

    def finish():
        half = lax.rem(fin_row, 2)
        xn = x_ref[...] + acc_ref[...]
        xrow_ref[fin_col] = xn
        part = jnp.broadcast_to(jnp.sum(xn * xn, axis=-1, keepdims=True), ssq_ref.shape[1:])
        ssq_ref[half] = ssq_ref[half] + part

    def emit():
        half = lax.rem(emi_row, 2)
        ssq = ssq_ref[half]
        scale = lax.rsqrt(ssq[:, :1] * (1.0 / d_model) + EPS)
        o_ref[...] = xrow_ref[emi_col] * scale * g_ref[...]
        ssq_ref[half] = ssq * jnp.where(emi_col == n_tiles - 1, 0.0, 1.0)

    @pl.when(t == 0)
    def _():
        ssq_ref[...] = jnp.zeros(ssq_ref.shape, F32)
        matmul()

    @pl.when(jnp.logical_and(t >= 1, t <= n_tiles))
    def _():
        finish()
        matmul()

    @pl.when(jnp.logical_and(t > n_tiles, t < n_mm))
    def _():
        emit()
        finish()
        matmul()

    @pl.when(t == n_mm)
    def _():
        emit()
        finish()

    @pl.when(t > n_mm)
    def _():
        emit()


def _outproj_final(a, r, w, x2, g):
    m, d = x2.shape
    ka, kr = a.shape[1], r.shape[1]
    assert ka == kr and w.shape[0] == ka + kr
    tm = min(PROJ_TM, m)
    tn = min(PROJ_TN, d)
    n_tiles = d // tn
    n_mm = (m // tm) * n_tiles
    n_steps = n_mm + n_tiles + 1
    assert n_mm > n_tiles

    def cur(t):
        t = jnp.minimum(t, n_mm - 1)
        return t // n_tiles, lax.rem(t, n_tiles)

    def fin(t):
        t = jnp.clip(t - 1, 0, n_mm - 1)
        return t // n_tiles, lax.rem(t, n_tiles)

    def emi(t):
        t = jnp.maximum(t - 1 - n_tiles, 0)
        return t // n_tiles, lax.rem(t, n_tiles)

    return pl.pallas_call(
        functools.partial(_outproj_final_kernel, n_tiles, n_mm, d),
        grid=(n_steps,),
        in_specs=[pl.BlockSpec((tm, ka), lambda t: (cur(t)[0], 0)),
                  pl.BlockSpec((tm, kr), lambda t: (cur(t)[0], 0)),
                  pl.BlockSpec((ka, tn), lambda t: (0, cur(t)[1])),
                  pl.BlockSpec((kr, tn), lambda t: (1, cur(t)[1])),
                  pl.BlockSpec((tm, tn), lambda t: fin(t)),
                  pl.BlockSpec((1, tn), lambda t: (0, emi(t)[1]))],
        out_specs=pl.BlockSpec((tm, tn), lambda t: emi(t)),
        out_shape=jax.ShapeDtypeStruct((m, d), F32),
        scratch_shapes=[pltpu.VMEM((tm, tn), F32),
                        pltpu.VMEM((n_tiles, tm, tn), F32),
                        pltpu.VMEM((2, tm, LANES), F32)],
        compiler_params=_cparams("arbitrary"),
        name="outproj_final",
    )(a, r, w, w, x2, g)


def _rope_tables(seq):
    rows = seq // GRID_W
    row = jnp.repeat(jnp.arange(rows), GRID_W).astype(F32)
    col = jnp.tile(jnp.arange(GRID_W), rows).astype(F32)
    axis_dim = HEAD_DIM // 2
    inv = ROPE_THETA ** (-jnp.arange(0, axis_dim, 2, dtype=F32) / axis_dim)
    ang_r = row[:, None] * inv[None, :]
    ang_c = col[:, None] * inv[None, :]
    cr, sr, cc, sc = jnp.cos(ang_r), jnp.sin(ang_r), jnp.cos(ang_c), jnp.sin(ang_c)
    z = jnp.zeros_like(sr)
    cos = jnp.concatenate([cr, cr, cc, cc], axis=-1)
    sin_up = jnp.concatenate([-sr, z, -sc, z], axis=-1)
    sin_dn = jnp.concatenate([z, sr, z, sc], axis=-1)
    return cos, sin_up, sin_dn


def kernel(x, norm_w, w_in, q_norm, k_norm, ret_decay_fwd, ret_decay_bwd, ret_norm, w_out, final_norm):
    batch, seq, d = x.shape
    depth = w_in.shape[0]
    attn_width = d // 2
    kv_width = attn_width // ATTN_GROUP
    ret_width = d - attn_width
    ret_heads = ret_width // RET_V_DIM
    ret_qk_width = ret_heads * RET_QK_DIM
    m = batch * seq

    ref_widths = (("aq", attn_width), ("ak", kv_width), ("av", kv_width), ("ag", attn_width),
                  ("rq", ret_qk_width), ("rk", ret_qk_width), ("rv", ret_width), ("rg", ret_width))
    ref_pos, pos = {}, 0
    for name, wdt in ref_widths:
        ref_pos[name] = (pos, wdt)
        pos += wdt
    assert pos == w_in.shape[2]
    tn = min(PROJ_TN, kv_width)
    offs, kind_ranges, gain_starts, pos = {}, [], [], 0
    for name in SECTION_ORDER:
        wdt = ref_pos[name][1]
        assert wdt % tn == 0
        offs[name] = pos
        kind = SECTION_KIND[name]
        if kind_ranges and kind_ranges[-1][0] == kind:
            kind_ranges[-1] = (kind, kind_ranges[-1][1], (pos + wdt) // tn)
        else:
            kind_ranges.append((kind, pos // tn, (pos + wdt) // tn))
        if name in GAINED:
            gain_starts.append(pos // tn)
        pos += wdt

    tile_shifts = tuple((offs[name] // tn, (offs[name] + ref_pos[name][1]) // tn, (ref_pos[name][0] - offs[name]) // tn)
                        for name in SECTION_ORDER)
    tables = _rope_tables(seq)
    attn_scale = HEAD_DIM ** -0.5
    x2 = x.reshape(m, d)

    xg, ssq = _prologue(x2, norm_w[0].reshape(1, d))
    for l in range(depth):
        gains = jnp.stack([q_norm[l] * (attn_scale * LOG2E),
                           k_norm[l],
                           jnp.ones((HEAD_DIM,), F32),
                           jnp.full((HEAD_DIM,), RET_QK_DIM ** -0.5, F32)]).reshape(len(GAINED), 1, HEAD_DIM)
        proj = _inproj(xg, w_in, l, ssq, tables, gains, seq=seq, kind_ranges=tuple(kind_ranges),
                       gain_starts=tuple(gain_starts), tile_shifts=tile_shifts, tn=tn)
        a, w_out_b = _attention(proj, gains.reshape(len(GAINED), HEAD_DIM), w_out, l, batch=batch, seq=seq,
                                attn_width=attn_width, kv_width=kv_width, offs=offs)
        dfw = jnp.broadcast_to(ret_decay_fwd[l][:, None, None], (ret_heads, 1, LANES))
        dbw = jnp.broadcast_to(ret_decay_bwd[l][:, None, None], (ret_heads, 1, LANES))
        r = _retention(proj, dfw, dbw, ret_norm[l].reshape(ret_heads, 1, RET_V_DIM),
                       batch=batch, seq=seq, offs=offs, ret_heads=ret_heads)
        if l + 1 < depth:
            x2, xg, ssq = _outproj(a, r, w_out_b, x2, norm_w[l + 1].reshape(1, d))
        else:
            out = _outproj_final(a, r, w_out_b, x2, final_norm.reshape(1, d))
    return out.reshape(batch, seq, d)
```

```python
import functools
import math

import jax
import jax.numpy as jnp
from jax import lax
from jax.experimental import pallas as pl
from jax.experimental.pallas import tpu as pltpu

F32 = jnp.float32
BF16 = jnp.bfloat16

GRID_W = 64
HEAD_DIM = 128
ATTN_GROUP = 4
RET_V_DIM = 256
RET_QK_DIM = 128
ROPE_THETA = 10000.0
EPS = 1e-6
LOG2E = math.log2(math.e)

LANES = 128
BF16_ROWS = 16
VMEM_LIMIT = 56 * 1024 * 1024

PROJ_TM = 1024
PROJ_TN = 512
ATTN_TQ = 256
ATTN_TK = 1024
RET_CHUNK = 256
ROWWISE_TM = 256

NT_DIMS = (((1,), (1,)), ((), ()))


def _cparams(*sem):
    return pltpu.CompilerParams(dimension_semantics=sem, vmem_limit_bytes=VMEM_LIMIT)


def _prologue_kernel(x_ref, g_ref, xg_ref, ssq_ref):
    x = x_ref[...]
    xg_ref[...] = (x * g_ref[...]).astype(BF16)
    ssq_ref[...] = jnp.broadcast_to(jnp.sum(x * x, axis=-1, keepdims=True), ssq_ref.shape)


def _prologue(x2, g):
    m, d = x2.shape
    tm = min(ROWWISE_TM, m)
    return pl.pallas_call(
        _prologue_kernel,
        grid=(m // tm,),
        in_specs=[pl.BlockSpec((tm, d), lambda i: (i, 0)),
                  pl.BlockSpec((1, d), lambda i: (0, 0))],
        out_specs=[pl.BlockSpec((tm, d), lambda i: (i, 0)),
                   pl.BlockSpec((tm, LANES), lambda i: (i, 0))],
        out_shape=[jax.ShapeDtypeStruct((m, d), BF16),
                   jax.ShapeDtypeStruct((m, LANES), F32)],
        compiler_params=_cparams("parallel"),
        name="prologue",
    )(x2, g)


SECTION_ORDER = ("ag", "rg", "aq", "ak", "rq", "rk", "av", "rv")
SECTION_KIND = {"ag": "gate", "rg": "gate", "aq": "norm_rope", "ak": "norm_rope",
                "rq": "rope", "rk": "rope", "av": "plain", "rv": "plain"}
GAINED = ("aq", "ak", "rq", "rk")
ROW_CHUNK = 512


def _rope(y, cos, sin_up, sin_dn):
    q = HEAD_DIM // 4
    return y * cos + pltpu.roll(y, HEAD_DIM - q, 1) * sin_up + pltpu.roll(y, q, 1) * sin_dn


def _silu(y):
    h = 0.5 * y
    return h + h * jnp.tanh(h)


def _inproj_kernel(kind_ranges, n_rows, n_steps, d_model, xg_ref, w_ref, ssq_ref, cos_ref, sup_ref, sdn_ref,
                   gain_ref, o_ref, acc_ref):
    t = pl.program_id(0)
    prev_col = jnp.maximum(t - 1, 0) // n_rows
    tm, tn = acc_ref.shape

    def matmul():
        acc_ref[...] = jnp.dot(xg_ref[...], w_ref[...].astype(BF16), preferred_element_type=F32)

    def postprocess(kind):
        for c in range(tm // ROW_CHUNK):
            rows = slice(c * ROW_CHUNK, (c + 1) * ROW_CHUNK)
            r = lax.rsqrt(ssq_ref[rows, :] * (1.0 / d_model) + EPS)
            for g in range(tn // LANES):
                cols = slice(g * LANES, (g + 1) * LANES)
                y = acc_ref[rows, cols] * r
                if kind == "norm_rope":
                    ms = jnp.mean(y * y, axis=-1, keepdims=True)
                    y = _rope(y * lax.rsqrt(ms + EPS) * gain_ref[...],
                              cos_ref[rows, :], sup_ref[rows, :], sdn_ref[rows, :])
                elif kind == "rope":
                    y = _rope(y * gain_ref[...], cos_ref[rows, :], sup_ref[rows, :], sdn_ref[rows, :])
                elif kind == "gate":
                    y = _silu(y)
                o_ref[rows, cols] = y.astype(BF16)

    @pl.when(t == 0)
    def _():
        matmul()

    middle = jnp.logical_and(t > 0, t < n_steps - 1)
    for kind, lo, hi in kind_ranges:
        @pl.when(jnp.logical_and(middle, jnp.logical_and(prev_col >= lo, prev_col < hi)))
        def _(kind=kind):
            postprocess(kind)
            matmul()

    @pl.when(t == n_steps - 1)
    def _():
        postprocess(kind_ranges[-1][0])


def _inproj(xg, w, layer, ssq, tables, gains, *, seq, kind_ranges, gain_starts, tile_shifts, tn):
    m, d = xg.shape
    n = w.shape[2]
    n_tiles = n // tn
    tm = min(PROJ_TM, seq)
    sb = seq // tm
    n_rows = m // tm
    n_steps = n_rows * n_tiles + 1
    assert kind_ranges[-1][2] == n_tiles and tm % ROW_CHUNK == 0

    def cur(t):
        t = jnp.minimum(t, n_steps - 2)
        return lax.rem(t, n_rows), t // n_rows

    def prev(t):
        t = jnp.maximum(t - 1, 0)
        return lax.rem(t, n_rows), t // n_rows

    def gain_index(t):
        col = prev(t)[1]
        return (sum((col >= s).astype(jnp.int32) for s in gain_starts[1:]), 0, 0)

    def weight_index(t):
        col = cur(t)[1]
        shift = sum(jnp.where(jnp.logical_and(col >= lo, col < hi), sh, 0) for lo, hi, sh in tile_shifts)
        return (layer, 0, col + shift)

    kern = functools.partial(_inproj_kernel, kind_ranges, n_rows, n_steps, d)
    tab = pl.BlockSpec((tm, LANES), lambda t: (lax.rem(prev(t)[0], sb), 0))
    return pl.pallas_call(
        kern,
        grid=(n_steps,),
        in_specs=[pl.BlockSpec((tm, d), lambda t: (cur(t)[0], 0)),
                  pl.BlockSpec((None, d, tn), weight_index),
                  pl.BlockSpec((tm, LANES), lambda t: (prev(t)[0], 0)),
                  tab, tab, tab,
                  pl.BlockSpec((None, 1, LANES), gain_index)],
        out_specs=pl.BlockSpec((tm, tn), lambda t: prev(t)),
        out_shape=jax.ShapeDtypeStruct((m, n), BF16),
        scratch_shapes=[pltpu.VMEM((tm, tn), F32)],
        compiler_params=_cparams("arbitrary"),
        name="inproj",
    )(xg, w, ssq, *tables, gains)


def _transpose_via_mxu(a):
    rows = lax.broadcasted_iota(jnp.int32, (LANES, LANES), 0)
    cols = lax.broadcasted_iota(jnp.int32, (LANES, LANES), 1)
    eye = jnp.where(rows == cols, 1.0, 0.0).astype(BF16)
    return lax.dot_general(eye, a, NT_DIMS, preferred_element_type=F32).astype(BF16)


SCORE_LIMIT = 50.0
Q_UNROLL = 4


def _attn_kernel(tq, tk, q_ref, k_ref, v_ref, g_ref, gains_ref, w_ref, o_ref, wb_ref, vt_ref):
    wb_ref[...] = w_ref[...].astype(BF16)
    seq = q_ref.shape[0]
    nq = ATTN_GROUP * tq
    vt_ref[...] = _transpose_via_mxu(v_ref[...])
    gain_q = jnp.max(jnp.abs(gains_ref[0:1, :]))
    gain_k = jnp.max(jnp.abs(gains_ref[1:2, :]))
    bound = gain_q * gain_k * (HEAD_DIM * 1.02)

    def load_q(qi):
        rows = pl.ds(pl.multiple_of(qi * tq, tq), tq)
        qb = q_ref[rows, :]
        qs = jnp.concatenate([qb[:, g * LANES:(g + 1) * LANES] for g in range(ATTN_GROUP)], axis=0)
        return rows, qs

    def finish(rows, o):
        for g in range(ATTN_GROUP):
            cols = slice(g * LANES, (g + 1) * LANES)
            og = o[:, g * tq:(g + 1) * tq].T
            o_ref[rows, cols] = (g_ref[rows, cols].astype(F32) * og).astype(BF16)

    def unshifted():
        def qblock(qi, carry):
            rows, qs = load_q(qi)

            def kvstep(c, st):
                l, acc = st
                c0 = pl.multiple_of(c * tk, tk)
                s = lax.dot_general(k_ref[pl.ds(c0, tk), :], qs, NT_DIMS,
                                    preferred_element_type=F32)
                p = jnp.exp2(s)
                l = l + jnp.sum(p.reshape(tk // 8, 8, nq), axis=0)
                return l, acc + jnp.dot(vt_ref[:, pl.ds(c0, tk)], p.astype(BF16), preferred_element_type=F32)

            init = (jnp.zeros((8, nq), F32), jnp.zeros((HEAD_DIM, nq), F32))
            l, acc = lax.fori_loop(0, seq // tk, kvstep, init, unroll=True)
            finish(rows, acc * (1.0 / jnp.sum(l, axis=0, keepdims=True)))
            return carry

        lax.fori_loop(0, seq // tq, qblock, 0, unroll=Q_UNROLL)

    def running_max():
        def qblock(qi, carry):
            rows, qs = load_q(qi)

            def kvstep(c, st):
                m, l, acc = st
                c0 = pl.multiple_of(c * tk, tk)
                s = lax.dot_general(k_ref[pl.ds(c0, tk), :], qs, NT_DIMS,
                                    preferred_element_type=F32)
                m_new = jnp.maximum(m, jnp.max(s, axis=0, keepdims=True))
                alpha = jnp.exp2(m - m_new)
                p = jnp.exp2(s - m_new)
                l = alpha * l + jnp.sum(p, axis=0, keepdims=True)
                pv = jnp.dot(vt_ref[:, pl.ds(c0, tk)], p.astype(BF16), preferred_element_type=F32)
                return m_new, l, alpha * acc + pv

            init = (jnp.full((1, nq), -1e30, F32), jnp.zeros((1, nq), F32), jnp.zeros((HEAD_DIM, nq), F32))
            _, l, acc = lax.fori_loop(0, seq // tk, kvstep, init)
            finish(rows, acc * (1.0 / l))
            return carry

        lax.fori_loop(0, seq // tq, qblock, 0)

    lax.cond(bound <= SCORE_LIMIT, unshifted, running_max)


def _attention(proj, gains, w_out, layer, *, batch, seq, attn_width, kv_width, offs):
    kvh = kv_width // HEAD_DIM
    wrows, wcols = w_out.shape[1:]
    slab = wrows // (batch * kvh)
    assert slab * batch * kvh == wrows and slab % BF16_ROWS == 0
    gw = ATTN_GROUP * HEAD_DIM
    tq = min(ATTN_TQ, seq)
    tk = min(ATTN_TK, seq)
    assert offs["aq"] % gw == 0 and offs["ag"] % gw == 0
    q_off = offs["aq"] // gw
    k_off = offs["ak"] // HEAD_DIM
    v_off = offs["av"] // HEAD_DIM
    g_off = offs["ag"] // gw
    return pl.pallas_call(
        functools.partial(_attn_kernel, tq, tk),
        grid=(batch, kvh),
        in_specs=[pl.BlockSpec((seq, gw), lambda b, h: (b, q_off + h)),
                  pl.BlockSpec((seq, HEAD_DIM), lambda b, h: (b, k_off + h)),
                  pl.BlockSpec((seq, HEAD_DIM), lambda b, h: (b, v_off + h)),
                  pl.BlockSpec((seq, gw), lambda b, h: (b, g_off + h)),
                  pl.BlockSpec(gains.shape, lambda b, h: (0, 0)),
                  pl.BlockSpec((None, slab, wcols), lambda b, h: (layer, b * kvh + h, 0))],
        out_specs=[pl.BlockSpec((seq, gw), lambda b, h: (b, h)),
                   pl.BlockSpec((slab, wcols), lambda b, h: (b * kvh + h, 0))],
        out_shape=[jax.ShapeDtypeStruct((batch * seq, attn_width), BF16),
                   jax.ShapeDtypeStruct((wrows, wcols), BF16)],
        scratch_shapes=[pltpu.VMEM((HEAD_DIM, seq), BF16)],
        compiler_params=_cparams("parallel", "parallel"),
        name="attention",
    )(proj, proj, proj, proj, gains, w_out)


RET_UNROLL = 16
RET_HEADS_PER_STEP = 2


def _log_sigmoid(x):
    return jnp.minimum(x, 0.0) - jnp.log1p(jnp.exp(-jnp.abs(x)))


def _retention_kernel(ck, q_ref, k_ref, v_ref, g_ref, df_ref, db_ref, gn_ref, o_ref, kt_ref, sb_ref):
    seq = q_ref.shape[0]
    nchunk = seq // ck
    heads = range(RET_HEADS_PER_STEP)
    qk = lambda e: slice(e * RET_QK_DIM, (e + 1) * RET_QK_DIM)
    vv = lambda e: slice(e * RET_V_DIM, (e + 1) * RET_V_DIM)

    diff = (lax.broadcasted_iota(jnp.int32, (ck, ck), 0)
            - lax.broadcasted_iota(jnp.int32, (ck, ck), 1)).astype(F32)
    rowi = lax.broadcasted_iota(jnp.int32, (ck, RET_QK_DIM), 0).astype(F32)
    coli = lax.broadcasted_iota(jnp.int32, (RET_QK_DIM, ck), 1).astype(F32)
    decay, xi_f, xi_b, zt_f, zt_b, gc_f, gc_b = [], [], [], [], [], [], []
    for e in heads:
        lgf = _log_sigmoid(df_ref[e])[:, :1]
        lgb = _log_sigmoid(db_ref[e])[:, :1]
        kt_ref[e] = _transpose_via_mxu(k_ref[:, qk(e)])
        decay.append(jnp.exp(jnp.where(diff >= 0, lgf * diff, -lgb * diff)))
        xi_f.append(jnp.exp(lgf * (rowi + 1.0)))
        xi_b.append(jnp.exp(lgb * (ck - rowi)))
        zt_f.append(jnp.exp(lgf * (ck - 1.0 - coli)))
        zt_b.append(jnp.exp(lgb * coli))
        gc_f.append(jnp.exp(lgf * ck))
        gc_b.append(jnp.exp(lgb * ck))

    def kz(e, c0, zt):
        return (kt_ref[e, :, pl.ds(c0, ck)].astype(F32) * zt).astype(BF16)

    def bstep(t, states):
        jc = nchunk - 1 - t
        c0 = pl.multiple_of(jc * ck, ck)
        new = []
        for e in heads:
            sb_ref[e, jc] = states[e].astype(BF16)
            upd = jnp.dot(kz(e, c0, zt_b[e]), v_ref[pl.ds(c0, ck), vv(e)], preferred_element_type=F32)
            new.append(gc_b[e] * states[e] + upd)
        return tuple(new)

    zeros = tuple(jnp.zeros((RET_QK_DIM, RET_V_DIM), F32) for _ in heads)
    lax.fori_loop(0, nchunk, bstep, zeros, unroll=RET_UNROLL)

    def fstep(jc, states):
        c0 = pl.multiple_of(jc * ck, ck)
        rows = pl.ds(c0, ck)
        new = []
        for e in heads:
            q = q_ref[rows, qk(e)]
            v = v_ref[rows, vv(e)]
            s = jnp.dot(q, kt_ref[e, :, pl.ds(c0, ck)], preferred_element_type=F32)
            out = jnp.dot((s * decay[e]).astype(BF16), v, preferred_element_type=F32)
            qf = q.astype(F32)
            qx = jnp.concatenate([(qf * xi_f[e]).astype(BF16), (qf * xi_b[e]).astype(BF16)], axis=1)
            both = jnp.concatenate([states[e].astype(BF16), sb_ref[e, jc]], axis=0)
            out = out + jnp.dot(qx, both, preferred_element_type=F32)
            ms = jnp.mean(out * out, axis=-1, keepdims=True)
            y = out * lax.rsqrt(ms + EPS) * gn_ref[e]
            o_ref[rows, vv(e)] = (g_ref[rows, vv(e)].astype(F32) * y).astype(BF16)
            upd = jnp.dot(kz(e, c0, zt_f[e]), v, preferred_element_type=F32)
            new.append(gc_f[e] * states[e] + upd)
        return tuple(new)

    lax.fori_loop(0, nchunk, fstep, zeros, unroll=RET_UNROLL)


def _retention(proj, dfw, dbw, gnorm, *, batch, seq, offs, ret_heads):
    hps = RET_HEADS_PER_STEP
    wq, wv = hps * RET_QK_DIM, hps * RET_V_DIM
    off_q, off_k, off_v, off_g = offs["rq"], offs["rk"], offs["rv"], offs["rg"]
    assert ret_heads % hps == 0 and off_q % wq == 0 and off_k % wq == 0 and off_v % wv == 0 and off_g % wv == 0
    ck = min(RET_CHUNK, seq)
    vec = lambda w: pl.BlockSpec((hps, 1, w), lambda b, h: (h, 0, 0))
    return pl.pallas_call(
        functools.partial(_retention_kernel, ck),
        grid=(batch, ret_heads // hps),
        in_specs=[pl.BlockSpec((seq, wq), lambda b, h: (b, off_q // wq + h)),
                  pl.BlockSpec((seq, wq), lambda b, h: (b, off_k // wq + h)),
                  pl.BlockSpec((seq, wv), lambda b, h: (b, off_v // wv + h)),
                  pl.BlockSpec((seq, wv), lambda b, h: (b, off_g // wv + h)),
                  vec(LANES), vec(LANES), vec(RET_V_DIM)],
        out_specs=pl.BlockSpec((seq, wv), lambda b, h: (b, h)),
        out_shape=jax.ShapeDtypeStruct((batch * seq, ret_heads * RET_V_DIM), BF16),
        scratch_shapes=[pltpu.VMEM((hps, RET_QK_DIM, seq), BF16),
                        pltpu.VMEM((hps, seq // ck, RET_QK_DIM, RET_V_DIM), BF16)],
        compiler_params=_cparams("parallel", "parallel"),
        name="retention",
    )(proj, proj, proj, proj, dfw, dbw, gnorm)


def _outproj_kernel(n_tiles, n_steps, a_ref, r_ref, wa_ref, wr_ref, x_ref, g_ref, xo_ref, xg_ref, ssq_ref, acc_ref):
    t = pl.program_id(0)
    prev_col = lax.rem(t + (n_tiles - 1), n_tiles)

    def matmul():
        acc = jnp.dot(a_ref[...], wa_ref[...], preferred_element_type=F32)
        acc_ref[...] = acc + jnp.dot(r_ref[...], wr_ref[...], preferred_element_type=F32)

    def finish():
        xn = x_ref[...] + acc_ref[...]
        xo_ref[...] = xn
        xg_ref[...] = (xn * g_ref[...]).astype(BF16)
        return jnp.broadcast_to(jnp.sum(xn * xn, axis=-1, keepdims=True), ssq_ref.shape)

    def accumulate(part):
        @pl.when(prev_col == 0)
        def _():
            ssq_ref[...] = part

        @pl.when(prev_col > 0)
        def _():
            ssq_ref[...] = ssq_ref[...] + part

    @pl.when(t == 0)
    def _():
        matmul()

    @pl.when(jnp.logical_and(t > 0, t < n_steps - 1))
    def _():
        part = finish()
        matmul()
        accumulate(part)

    @pl.when(t == n_steps - 1)
    def _():
        accumulate(finish())


def _outproj(a, r, w, x2, gnext):
    m, d = x2.shape
    ka, kr = a.shape[1], r.shape[1]
    assert ka == kr and w.shape[0] == ka + kr
    tm = min(PROJ_TM, m)
    tn = min(PROJ_TN, d)
    n_tiles = d // tn
    n_steps = (m // tm) * n_tiles + 1

    def cur(t):
        t = jnp.minimum(t, n_steps - 2)
        return t // n_tiles, lax.rem(t, n_tiles)

    def prev(t):
        t = jnp.maximum(t - 1, 0)
        return t // n_tiles, lax.rem(t, n_tiles)

    return pl.pallas_call(
        functools.partial(_outproj_kernel, n_tiles, n_steps),
        grid=(n_steps,),
        in_specs=[pl.BlockSpec((tm, ka), lambda t: (cur(t)[0], 0)),
                  pl.BlockSpec((tm, kr), lambda t: (cur(t)[0], 0)),
                  pl.BlockSpec((ka, tn), lambda t: (0, cur(t)[1])),
                  pl.BlockSpec((kr, tn), lambda t: (1, cur(t)[1])),
                  pl.BlockSpec((tm, tn), lambda t: prev(t)),
                  pl.BlockSpec((1, tn), lambda t: (0, prev(t)[1]))],
        out_specs=[pl.BlockSpec((tm, tn), lambda t: prev(t)),
                   pl.BlockSpec((tm, tn), lambda t: prev(t)),
                   pl.BlockSpec((tm, LANES), lambda t: (prev(t)[0], 0))],
        out_shape=[jax.ShapeDtypeStruct((m, d), F32),
                   jax.ShapeDtypeStruct((m, d), BF16),
                   jax.ShapeDtypeStruct((m, LANES), F32)],
        scratch_shapes=[pltpu.VMEM((tm, tn), F32)],
        compiler_params=_cparams("arbitrary"),
        name="outproj",
    )(a, r, w, w, x2, gnext)


def _outproj_final_kernel(n_tiles, n_mm, d_model, a_ref, r_ref, wa_ref, wr_ref, x_ref, g_ref, o_ref,
                          acc_ref, xrow_ref, ssq_ref):
    t = pl.program_id(0)
    fin = jnp.maximum(t - 1, 0)
    fin_row, fin_col = fin // n_tiles, lax.rem(fin, n_tiles)
    emi = jnp.maximum(t - 1 - n_tiles, 0)
    emi_row, emi_col = emi // n_tiles, lax.rem(emi, n_tiles)

    def matmul():
        acc = jnp.dot(a_ref[...], wa_ref[...], preferred_element_type=F32)
        acc_ref[...] = acc + jnp.dot(r_ref[...], wr_ref[...], preferred_element_type=F32)

    def finish():
        half = lax.rem(fin_row, 2)
        xn = x_ref[...] + acc_ref[...]
        xrow_ref[fin_col] = xn
        part = jnp.broadcast_to(jnp.sum(xn * xn, axis=-1, keepdims=True), ssq_ref.shape[1:])
        ssq_ref[half] = ssq_ref[half] + part

    def emit():
        half = lax.rem(emi_row, 2)
        ssq = ssq_ref[half]
        scale = lax.rsqrt(ssq[:, :1] * (1.0 / d_model) + EPS)
        o_ref[...] = xrow_ref[emi_col] * scale * g_ref[...]
        ssq_ref[half] = ssq * jnp.where(emi_col == n_tiles - 1, 0.0, 1.0)

    @pl.when(t == 0)
    def _():
        ssq_ref[...] = jnp.zeros(ssq_ref.shape, F32)
        matmul()

    @pl.when(jnp.logical_and(t >= 1, t <= n_tiles))
    def _():
        finish()
        matmul()

    @pl.when(jnp.logical_and(t > n_tiles, t < n_mm))
    def _():
        emit()
        finish()
        matmul()

    @pl.when(t == n_mm)
    def _():
        emit()
        finish()

    @pl.when(t > n_mm)
    def _():
        emit()


def _outproj_final(a, r, w, x2, g):
    m, d = x2.shape
    ka, kr = a.shape[1], r.shape[1]
    assert ka == kr and w.shape[0] == ka + kr
    tm = min(PROJ_TM, m)
    tn = min(PROJ_TN, d)
    n_tiles = d // tn
    n_mm = (m // tm) * n_tiles
    n_steps = n_mm + n_tiles + 1
    assert n_mm > n_tiles

    def cur(t):
        t = jnp.minimum(t, n_mm - 1)
        return t // n_tiles, lax.rem(t, n_tiles)

    def fin(t):
        t = jnp.clip(t - 1, 0, n_mm - 1)
        return t // n_tiles, lax.rem(t, n_tiles)

    def emi(t):
        t = jnp.maximum(t - 1 - n_tiles, 0)
        return t // n_tiles, lax.rem(t, n_tiles)

    return pl.pallas_call(
        functools.partial(_outproj_final_kernel, n_tiles, n_mm, d),
        grid=(n_steps,),
        in_specs=[pl.BlockSpec((tm, ka), lambda t: (cur(t)[0], 0)),
                  pl.BlockSpec((tm, kr), lambda t: (cur(t)[0], 0)),
                  pl.BlockSpec((ka, tn), lambda t: (0, cur(t)[1])),
                  pl.BlockSpec((kr, tn), lambda t: (1, cur(t)[1])),
                  pl.BlockSpec((tm, tn), lambda t: fin(t)),
                  pl.BlockSpec((1, tn), lambda t: (0, emi(t)[1]))],
        out_specs=pl.BlockSpec((tm, tn), lambda t: emi(t)),
        out_shape=jax.ShapeDtypeStruct((m, d), F32),
        scratch_shapes=[pltpu.VMEM((tm, tn), F32),
                        pltpu.VMEM((n_tiles, tm, tn), F32),
                        pltpu.VMEM((2, tm, LANES), F32)],
        compiler_params=_cparams("arbitrary"),
        name="outproj_final",
    )(a, r, w, w, x2, g)


def _rope_tables(seq):
    rows = seq // GRID_W
    row = jnp.repeat(jnp.arange(rows), GRID_W).astype(F32)
    col = jnp.tile(jnp.arange(GRID_W), rows).astype(F32)
    axis_dim = HEAD_DIM // 2
    inv = ROPE_THETA ** (-jnp.arange(0, axis_dim, 2, dtype=F32) / axis_dim)
    ang_r = row[:, None] * inv[None, :]
    ang_c = col[:, None] * inv[None, :]
    cr, sr, cc, sc = jnp.cos(ang_r), jnp.sin(ang_r), jnp.cos(ang_c), jnp.sin(ang_c)
    z = jnp.zeros_like(sr)
    cos = jnp.concatenate([cr, cr, cc, cc], axis=-1)
    sin_up = jnp.concatenate([-sr, z, -sc, z], axis=-1)
    sin_dn = jnp.concatenate([z, sr, z, sc], axis=-1)
    return cos, sin_up, sin_dn


def kernel(x, norm_w, w_in, q_norm, k_norm, ret_decay_fwd, ret_decay_bwd, ret_norm, w_out, final_norm):
    batch, seq, d = x.shape
    depth = w_in.shape[0]
    attn_width = d // 2
    kv_width = attn_width // ATTN_GROUP
    ret_width = d - attn_width
    ret_heads = ret_width // RET_V_DIM
    ret_qk_width = ret_heads * RET_QK_DIM
    m = batch * seq

    ref_widths = (("aq", attn_width), ("ak", kv_width), ("av", kv_width), ("ag", attn_width),
                  ("rq", ret_qk_width), ("rk", ret_qk_width), ("rv", ret_width), ("rg", ret_width))
    ref_pos, pos = {}, 0
    for name, wdt in ref_widths:
        ref_pos[name] = (pos, wdt)
        pos += wdt
    assert pos == w_in.shape[2]
    tn = min(PROJ_TN, kv_width)
    offs, kind_ranges, gain_starts, pos = {}, [], [], 0
    for name in SECTION_ORDER:
        wdt = ref_pos[name][1]
        assert wdt % tn == 0
        offs[name] = pos
        kind = SECTION_KIND[name]
        if kind_ranges and kind_ranges[-1][0] == kind:
            kind_ranges[-1] = (kind, kind_ranges[-1][1], (pos + wdt) // tn)
        else:
            kind_ranges.append((kind, pos // tn, (pos + wdt) // tn))
        if name in GAINED:
            gain_starts.append(pos // tn)
        pos += wdt

    tile_shifts = tuple((offs[name] // tn, (offs[name] + ref_pos[name][1]) // tn, (ref_pos[name][0] - offs[name]) // tn)
                        for name in SECTION_ORDER)
    tables = _rope_tables(seq)
    attn_scale = HEAD_DIM ** -0.5
    x2 = x.reshape(m, d)

    xg, ssq = _prologue(x2, norm_w[0].reshape(1, d))
    for l in range(depth):
        gains = jnp.stack([q_norm[l] * (attn_scale * LOG2E),
                           k_norm[l],
                           jnp.ones((HEAD_DIM,), F32),
                           jnp.full((HEAD_DIM,), RET_QK_DIM ** -0.5, F32)]).reshape(len(GAINED), 1, HEAD_DIM)
        proj = _inproj(xg, w_in, l, ssq, tables, gains, seq=seq, kind_ranges=tuple(kind_ranges),
                       gain_starts=tuple(gain_starts), tile_shifts=tile_shifts, tn=tn)
        a, w_out_b = _attention(proj, gains.reshape(len(GAINED), HEAD_DIM), w_out, l, batch=batch, seq=seq,
                                attn_width=attn_width, kv_width=kv_width, offs=offs)
        dfw = jnp.broadcast_to(ret_decay_fwd[l][:, None, None], (ret_heads, 1, LANES))
        dbw = jnp.broadcast_to(ret_decay_bwd[l][:, None, None], (ret_heads, 1, LANES))
        r = _retention(proj, dfw, dbw, ret_norm[l].reshape(ret_heads, 1, RET_V_DIM),
                       batch=batch, seq=seq, offs=offs, ret_heads=ret_heads)
        if l + 1 < depth:
            x2, xg, ssq = _outproj(a, r, w_out_b, x2, norm_w[l + 1].reshape(1, d))
        else:
            out = _outproj_final(a, r, w_out_b, x2, final_norm.reshape(1, d))
    return out.reshape(batch, seq, d)
```

```python
import functools
import math

import jax
import jax.numpy as jnp
from jax import lax
from jax.experimental import pallas as pl
from jax.experimental.pallas import tpu as pltpu

F32 = jnp.float32
BF16 = jnp.bfloat16

GRID_W = 64
HEAD_DIM = 128
ATTN_GROUP = 4
RET_V_DIM = 256
RET_QK_DIM = 128
ROPE_THETA = 10000.0
EPS = 1e-6
LOG2E = math.log2(math.e)

LANES = 128
BF16_ROWS = 16
VMEM_LIMIT = 56 * 1024 * 1024

PROJ_TM = 1024
PROJ_TN = 512
ATTN_TQ = 256
ATTN_TK = 1024
RET_CHUNK = 256
ROWWISE_TM = 256

NT_DIMS = (((1,), (1,)), ((), ()))


def _cparams(*sem):
    return pltpu.CompilerParams(dimension_semantics=sem, vmem_limit_bytes=VMEM_LIMIT)


def _prologue_kernel(x_ref, g_ref, xg_ref, ssq_ref):
    x = x_ref[...]
    xg_ref[...] = (x * g_ref[...]).astype(BF16)
    ssq_ref[...] = jnp.broadcast_to(jnp.sum(x * x, axis=-1, keepdims=True), ssq_ref.shape)


def _prologue(x2, g):
    m, d = x2.shape
    tm = min(ROWWISE_TM, m)
    return pl.pallas_call(
        _prologue_kernel,
        grid=(m // tm,),
        in_specs=[pl.BlockSpec((tm, d), lambda i: (i, 0)),
                  pl.BlockSpec((1, d), lambda i: (0, 0))],
        out_specs=[pl.BlockSpec((tm, d), lambda i: (i, 0)),
                   pl.BlockSpec((tm, LANES), lambda i: (i, 0))],
        out_shape=[jax.ShapeDtypeStruct((m, d), BF16),
                   jax.ShapeDtypeStruct((m, LANES), F32)],
        compiler_params=_cparams("parallel"),
        name="prologue",
    )(x2, g)


SECTION_ORDER = ("ag", "rg", "aq", "ak", "rq", "rk", "av", "rv")
SECTION_KIND = {"ag": "gate", "rg": "gate", "aq": "norm_rope", "ak": "norm_rope",
                "rq": "rope", "rk": "rope", "av": "plain", "rv": "plain"}
GAINED = ("aq", "ak", "rq", "rk")
ROW_CHUNK = 512


def _rope(y, cos, sin_up, sin_dn):
    q = HEAD_DIM // 4
    return y * cos + pltpu.roll(y, HEAD_DIM - q, 1) * sin_up + pltpu.roll(y, q, 1) * sin_dn


def _silu(y):
    h = 0.5 * y
    return h + h * jnp.tanh(h)


def _inproj_kernel(kind_ranges, n_rows, n_steps, d_model, xg_ref, w_ref, ssq_ref, cos_ref, sup_ref, sdn_ref,
                   gain_ref, o_ref, acc_ref, wb_ref):
    t = pl.program_id(0)
    prev_col = jnp.maximum(t - 1, 0) // n_rows
    tm, tn = acc_ref.shape

    @pl.when(jnp.logical_and(lax.rem(t, n_rows) == 0, t < n_steps - 1))
    def _():
        wb_ref[...] = w_ref[...].astype(BF16)

    def matmul():
        acc_ref[...] = jnp.dot(xg_ref[...], wb_ref[...], preferred_element_type=F32)

    def postprocess(kind):
        for c in range(tm // ROW_CHUNK):
            rows = slice(c * ROW_CHUNK, (c + 1) * ROW_CHUNK)
            r = lax.rsqrt(ssq_ref[rows, :] * (1.0 / d_model) + EPS)
            for g in range(tn // LANES):
                cols = slice(g * LANES, (g + 1) * LANES)
                y = acc_ref[rows, cols] * r
                if kind == "norm_rope":
                    ms = jnp.mean(y * y, axis=-1, keepdims=True)
                    y = _rope(y * lax.rsqrt(ms + EPS) * gain_ref[...],
                              cos_ref[rows, :], sup_ref[rows, :], sdn_ref[rows, :])
                elif kind == "rope":
                    y = _rope(y * gain_ref[...], cos_ref[rows, :], sup_ref[rows, :], sdn_ref[rows, :])
                elif kind == "gate":
                    y = _silu(y)
                o_ref[rows, cols] = y.astype(BF16)

    @pl.when(t == 0)
    def _():
        matmul()

    middle = jnp.logical_and(t > 0, t < n_steps - 1)
    for kind, lo, hi in kind_ranges:
        @pl.when(jnp.logical_and(middle, jnp.logical_and(prev_col >= lo, prev_col < hi)))
        def _(kind=kind):
            postprocess(kind)
            matmul()

    @pl.when(t == n_steps - 1)
    def _():
        postprocess(kind_ranges[-1][0])


def _inproj(xg, w, layer, ssq, tables, gains, *, seq, kind_ranges, gain_starts, tile_shifts, tn):
    m, d = xg.shape
    n = w.shape[2]
    n_tiles = n // tn
    tm = min(PROJ_TM, seq)
    sb = seq // tm
    n_rows = m // tm
    n_steps = n_rows * n_tiles + 1
    assert kind_ranges[-1][2] == n_tiles and tm % ROW_CHUNK == 0

    def cur(t):
        t = jnp.minimum(t, n_steps - 2)
        return lax.rem(t, n_rows), t // n_rows

    def prev(t):
        t = jnp.maximum(t - 1, 0)
        return lax.rem(t, n_rows), t // n_rows

    def gain_index(t):
        col = prev(t)[1]
        return (sum((col >= s).astype(jnp.int32) for s in gain_starts[1:]), 0, 0)

    def weight_index(t):
        col = cur(t)[1]
        shift = sum(jnp.where(jnp.logical_and(col >= lo, col < hi), sh, 0) for lo, hi, sh in tile_shifts)
        return (layer, 0, col + shift)

    kern = functools.partial(_inproj_kernel, kind_ranges, n_rows, n_steps, d)
    tab = pl.BlockSpec((tm, LANES), lambda t: (lax.rem(prev(t)[0], sb), 0))
    return pl.pallas_call(
        kern,
        grid=(n_steps,),
        in_specs=[pl.BlockSpec((tm, d), lambda t: (cur(t)[0], 0)),
                  pl.BlockSpec((None, d, tn), weight_index),
                  pl.BlockSpec((tm, LANES), lambda t: (prev(t)[0], 0)),
                  tab, tab, tab,
                  pl.BlockSpec((None, 1, LANES), gain_index)],
        out_specs=pl.BlockSpec((tm, tn), lambda t: prev(t)),
        out_shape=jax.ShapeDtypeStruct((m, n), BF16),
        scratch_shapes=[pltpu.VMEM((tm, tn), F32), pltpu.VMEM((d, tn), BF16)],
        compiler_params=_cparams("arbitrary"),
        name="inproj",
    )(xg, w, ssq, *tables, gains)


def _transpose_via_mxu(a):
    rows = lax.broadcasted_iota(jnp.int32, (LANES, LANES), 0)
    cols = lax.broadcasted_iota(jnp.int32, (LANES, LANES), 1)
    eye = jnp.where(rows == cols, 1.0, 0.0).astype(BF16)
    return lax.dot_general(eye, a, NT_DIMS, preferred_element_type=F32).astype(BF16)


SCORE_LIMIT = 50.0
Q_UNROLL = 4


def _attn_kernel(tq, tk, q_ref, k_ref, v_ref, g_ref, gains_ref, w_ref, o_ref, wb_ref, vt_ref):
    wb_ref[...] = w_ref[...].astype(BF16)
    seq = q_ref.shape[0]
    nq = ATTN_GROUP * tq
    vt_ref[...] = _transpose_via_mxu(v_ref[...])
    gain_q = jnp.max(jnp.abs(gains_ref[0:1, :]))
    gain_k = jnp.max(jnp.abs(gains_ref[1:2, :]))
    bound = gain_q * gain_k * (HEAD_DIM * 1.02)

    def load_q(qi):
        rows = pl.ds(pl.multiple_of(qi * tq, tq), tq)
        qb = q_ref[rows, :]
        qs = jnp.concatenate([qb[:, g * LANES:(g + 1) * LANES] for g in range(ATTN_GROUP)], axis=0)
        return rows, qs

    def finish(rows, o):
        for g in range(ATTN_GROUP):
            cols = slice(g * LANES, (g + 1) * LANES)
            og = o[:, g * tq:(g + 1) * tq].T
            o_ref[rows, cols] = (g_ref[rows, cols].astype(F32) * og).astype(BF16)

    def unshifted():
        def qblock(qi, carry):
            rows, qs = load_q(qi)

            def kvstep(c, st):
                l, acc = st
                c0 = pl.multiple_of(c * tk, tk)
                s = lax.dot_general(k_ref[pl.ds(c0, tk), :], qs, NT_DIMS,
                                    preferred_element_type=F32)
                p = jnp.exp2(s)
                l = l + jnp.sum(p.reshape(tk // 8, 8, nq), axis=0)
                return l, acc + jnp.dot(vt_ref[:, pl.ds(c0, tk)], p.astype(BF16), preferred_element_type=F32)

            init = (jnp.zeros((8, nq), F32), jnp.zeros((HEAD_DIM, nq), F32))
            l, acc = lax.fori_loop(0, seq // tk, kvstep, init, unroll=True)
            finish(rows, acc * (1.0 / jnp.sum(l, axis=0, keepdims=True)))
            return carry

        lax.fori_loop(0, seq // tq, qblock, 0, unroll=Q_UNROLL)

    def running_max():
        def qblock(qi, carry):
            rows, qs = load_q(qi)

            def kvstep(c, st):
                m, l, acc = st
                c0 = pl.multiple_of(c * tk, tk)
                s = lax.dot_general(k_ref[pl.ds(c0, tk), :], qs, NT_DIMS,
                                    preferred_element_type=F32)
                m_new = jnp.maximum(m, jnp.max(s, axis=0, keepdims=True))
                alpha = jnp.exp2(m - m_new)
                p = jnp.exp2(s - m_new)
                l = alpha * l + jnp.sum(p, axis=0, keepdims=True)
                pv = jnp.dot(vt_ref[:, pl.ds(c0, tk)], p.astype(BF16), preferred_element_type=F32)
                return m_new, l, alpha * acc + pv

            init = (jnp.full((1, nq), -1e30, F32), jnp.zeros((1, nq), F32), jnp.zeros((HEAD_DIM, nq), F32))
            _, l, acc = lax.fori_loop(0, seq // tk, kvstep, init)
            finish(rows, acc * (1.0 / l))
            return carry

        lax.fori_loop(0, seq // tq, qblock, 0)

    lax.cond(bound <= SCORE_LIMIT, unshifted, running_max)


def _attention(proj, gains, w_out, layer, *, batch, seq, attn_width, kv_width, offs):
    kvh = kv_width // HEAD_DIM
    wrows, wcols = w_out.shape[1:]
    slab = wrows // (batch * kvh)
    assert slab * batch * kvh == wrows and slab % BF16_ROWS == 0
    gw = ATTN_GROUP * HEAD_DIM
    tq = min(ATTN_TQ, seq)
    tk = min(ATTN_TK, seq)
    assert offs["aq"] % gw == 0 and offs["ag"] % gw == 0
    q_off = offs["aq"] // gw
    k_off = offs["ak"] // HEAD_DIM
    v_off = offs["av"] // HEAD_DIM
    g_off = offs["ag"] // gw
    return pl.pallas_call(
        functools.partial(_attn_kernel, tq, tk),
        grid=(batch, kvh),
        in_specs=[pl.BlockSpec((seq, gw), lambda b, h: (b, q_off + h)),
                  pl.BlockSpec((seq, HEAD_DIM), lambda b, h: (b, k_off + h)),
                  pl.BlockSpec((seq, HEAD_DIM), lambda b, h: (b, v_off + h)),
                  pl.BlockSpec((seq, gw), lambda b, h: (b, g_off + h)),
                  pl.BlockSpec(gains.shape, lambda b, h: (0, 0)),
                  pl.BlockSpec((None, slab, wcols), lambda b, h: (layer, b * kvh + h, 0))],
        out_specs=[pl.BlockSpec((seq, gw), lambda b, h: (b, h)),
                   pl.BlockSpec((slab, wcols), lambda b, h: (b * kvh + h, 0))],
        out_shape=[jax.ShapeDtypeStruct((batch * seq, attn_width), BF16),
                   jax.ShapeDtypeStruct((wrows, wcols), BF16)],
        scratch_shapes=[pltpu.VMEM((HEAD_DIM, seq), BF16)],
        compiler_params=_cparams("parallel", "parallel"),
        name="attention",
    )(proj, proj, proj, proj, gains, w_out)


RET_UNROLL = 16
RET_HEADS_PER_STEP = 2


def _log_sigmoid(x):
    return jnp.minimum(x, 0.0) - jnp.log1p(jnp.exp(-jnp.abs(x)))


def _retention_kernel(ck, q_ref, k_ref, v_ref, g_ref, df_ref, db_ref, gn_ref, o_ref, kt_ref, sb_ref):
    seq = q_ref.shape[0]
    nchunk = seq // ck
    heads = range(RET_HEADS_PER_STEP)
    qk = lambda e: slice(e * RET_QK_DIM, (e + 1) * RET_QK_DIM)
    vv = lambda e: slice(e * RET_V_DIM, (e + 1) * RET_V_DIM)

    diff = (lax.broadcasted_iota(jnp.int32, (ck, ck), 0)
            - lax.broadcasted_iota(jnp.int32, (ck, ck), 1)).astype(F32)
    rowi = lax.broadcasted_iota(jnp.int32, (ck, RET_QK_DIM), 0).astype(F32)
    coli = lax.broadcasted_iota(jnp.int32, (RET_QK_DIM, ck), 1).astype(F32)
    decay, xi_f, xi_b, zt_f, zt_b, gc_f, gc_b = [], [], [], [], [], [], []
    for e in heads:
        lgf = _log_sigmoid(df_ref[e])[:, :1]
        lgb = _log_sigmoid(db_ref[e])[:, :1]
        kt_ref[e] = _transpose_via_mxu(k_ref[:, qk(e)])
        decay.append(jnp.exp(jnp.where(diff >= 0, lgf * diff, -lgb * diff)))
        xi_f.append(jnp.exp(lgf * (rowi + 1.0)))
        xi_b.append(jnp.exp(lgb * (ck - rowi)))
        zt_f.append(jnp.exp(lgf * (ck - 1.0 - coli)))
        zt_b.append(jnp.exp(lgb * coli))
        gc_f.append(jnp.exp(lgf * ck))
        gc_b.append(jnp.exp(lgb * ck))

    def kz(e, c0, zt):
        return (kt_ref[e, :, pl.ds(c0, ck)].astype(F32) * zt).astype(BF16)

    def bstep(t, states):
        jc = nchunk - 1 - t
        c0 = pl.multiple_of(jc * ck, ck)
        new = []
        for e in heads:
            sb_ref[e, jc] = states[e].astype(BF16)
            upd = jnp.dot(kz(e, c0, zt_b[e]), v_ref[pl.ds(c0, ck), vv(e)], preferred_element_type=F32)
            new.append(gc_b[e] * states[e] + upd)
        return tuple(new)

    zeros = tuple(jnp.zeros((RET_QK_DIM, RET_V_DIM), F32) for _ in heads)
    lax.fori_loop(0, nchunk, bstep, zeros, unroll=RET_UNROLL)

    def fstep(jc, states):
        c0 = pl.multiple_of(jc * ck, ck)
        rows = pl.ds(c0, ck)
        new = []
        for e in heads:
            q = q_ref[rows, qk(e)]
            v = v_ref[rows, vv(e)]
            s = jnp.dot(q, kt_ref[e, :, pl.ds(c0, ck)], preferred_element_type=F32)
            out = jnp.dot((s * decay[e]).astype(BF16), v, preferred_element_type=F32)
            qf = q.astype(F32)
            qx = jnp.concatenate([(qf * xi_f[e]).astype(BF16), (qf * xi_b[e]).astype(BF16)], axis=1)
            both = jnp.concatenate([states[e].astype(BF16), sb_ref[e, jc]], axis=0)
            out = out + jnp.dot(qx, both, preferred_element_type=F32)
            ms = jnp.mean(out * out, axis=-1, keepdims=True)
            y = out * lax.rsqrt(ms + EPS) * gn_ref[e]
            o_ref[rows, vv(e)] = (g_ref[rows, vv(e)].astype(F32) * y).astype(BF16)
            upd = jnp.dot(kz(e, c0, zt_f[e]), v, preferred_element_type=F32)
            new.append(gc_f[e] * states[e] + upd)
        return tuple(new)

    lax.fori_loop(0, nchunk, fstep, zeros, unroll=RET_UNROLL)


def _retention(proj, dfw, dbw, gnorm, *, batch, seq, offs, ret_heads):
    hps = RET_HEADS_PER_STEP
    wq, wv = hps * RET_QK_DIM, hps * RET_V_DIM
    off_q, off_k, off_v, off_g = offs["rq"], offs["rk"], offs["rv"], offs["rg"]
    assert ret_heads % hps == 0 and off_q % wq == 0 and off_k % wq == 0 and off_v % wv == 0 and off_g % wv == 0
    ck = min(RET_CHUNK, seq)
    vec = lambda w: pl.BlockSpec((hps, 1, w), lambda b, h: (h, 0, 0))
    return pl.pallas_call(
        functools.partial(_retention_kernel, ck),
        grid=(batch, ret_heads // hps),
        in_specs=[pl.BlockSpec((seq, wq), lambda b, h: (b, off_q // wq + h)),
                  pl.BlockSpec((seq, wq), lambda b, h: (b, off_k // wq + h)),
                  pl.BlockSpec((seq, wv), lambda b, h: (b, off_v // wv + h)),
                  pl.BlockSpec((seq, wv), lambda b, h: (b, off_g // wv + h)),
                  vec(LANES), vec(LANES), vec(RET_V_DIM)],
        out_specs=pl.BlockSpec((seq, wv), lambda b, h: (b, h)),
        out_shape=jax.ShapeDtypeStruct((batch * seq, ret_heads * RET_V_DIM), BF16),
        scratch_shapes=[pltpu.VMEM((hps, RET_QK_DIM, seq), BF16),
                        pltpu.VMEM((hps, seq // ck, RET_QK_DIM, RET_V_DIM), BF16)],
        compiler_params=_cparams("parallel", "parallel"),
        name="retention",
    )(proj, proj, proj, proj, dfw, dbw, gnorm)


def _outproj_kernel(n_tiles, n_steps, a_ref, r_ref, wa_ref, wr_ref, x_ref, g_ref, xo_ref, xg_ref, ssq_ref, acc_ref):
    t = pl.program_id(0)
    prev_col = lax.rem(t + (n_tiles - 1), n_tiles)

    def matmul():
        acc = jnp.dot(a_ref[...], wa_ref[...], preferred_element_type=F32)
        acc_ref[...] = acc + jnp.dot(r_ref[...], wr_ref[...], preferred_element_type=F32)

    def finish():
        xn = x_ref[...] + acc_ref[...]
        xo_ref[...] = xn
        xg_ref[...] = (xn * g_ref[...]).astype(BF16)
        return jnp.broadcast_to(jnp.sum(xn * xn, axis=-1, keepdims=True), ssq_ref.shape)

    def accumulate(part):
        @pl.when(prev_col == 0)
        def _():
            ssq_ref[...] = part

        @pl.when(prev_col > 0)
        def _():
            ssq_ref[...] = ssq_ref[...] + part

    @pl.when(t == 0)
    def _():
        matmul()

    @pl.when(jnp.logical_and(t > 0, t < n_steps - 1))
    def _():
        part = finish()
        matmul()
        accumulate(part)

    @pl.when(t == n_steps - 1)
    def _():
        accumulate(finish())


def _outproj(a, r, w, x2, gnext):
    m, d = x2.shape
    ka, kr = a.shape[1], r.shape[1]
    assert ka == kr and w.shape[0] == ka + kr
    tm = min(PROJ_TM, m)
    tn = min(PROJ_TN, d)
    n_tiles = d // tn
    n_steps = (m // tm) * n_tiles + 1

    def cur(t):
        t = jnp.minimum(t, n_steps - 2)
        return t // n_tiles, lax.rem(t, n_tiles)

    def prev(t):
        t = jnp.maximum(t - 1, 0)
        return t // n_tiles, lax.rem(t, n_tiles)

    return pl.pallas_call(
        functools.partial(_outproj_kernel, n_tiles, n_steps),
        grid=(n_steps,),
        in_specs=[pl.BlockSpec((tm, ka), lambda t: (cur(t)[0], 0)),
                  pl.BlockSpec((tm, kr), lambda t: (cur(t)[0], 0)),
                  pl.BlockSpec((ka, tn), lambda t: (0, cur(t)[1])),
                  pl.BlockSpec((kr, tn), lambda t: (1, cur(t)[1])),
                  pl.BlockSpec((tm, tn), lambda t: prev(t)),
                  pl.BlockSpec((1, tn), lambda t: (0, prev(t)[1]))],
        out_specs=[pl.BlockSpec((tm, tn), lambda t: prev(t)),
                   pl.BlockSpec((tm, tn), lambda t: prev(t)),
                   pl.BlockSpec((tm, LANES), lambda t: (prev(t)[0], 0))],
        out_shape=[jax.ShapeDtypeStruct((m, d), F32),
                   jax.ShapeDtypeStruct((m, d), BF16),
                   jax.ShapeDtypeStruct((m, LANES), F32)],
        scratch_shapes=[pltpu.VMEM((tm, tn), F32)],
        compiler_params=_cparams("arbitrary"),
        name="outproj",
    )(a, r, w, w, x2, gnext)


def _outproj_final_kernel(n_tiles, n_mm, d_model, a_ref, r_ref, wa_ref, wr_ref, x_ref, g_ref, o_ref,
                          acc_ref, xrow_ref, ssq_ref):
    t = pl.program_id(0)
    fin = jnp.maximum(t - 1, 0)
    fin_row, fin_col = fin // n_tiles, lax.rem(fin, n_tiles)
    emi = jnp.maximum(t - 1 - n_tiles, 0)
    emi_row, emi_col = emi // n_tiles, lax.rem(emi, n_tiles)

    def matmul():
        acc = jnp.dot(a_ref[...], wa_ref[...], preferred_element_type=F32)
        acc_ref[...] = acc + jnp.dot(r_ref[...], wr_ref[...], preferred_element_type=F32)

    def finish():
        half = lax.rem(fin_row, 2)
        xn = x_ref[...] + acc_ref[...]
        xrow_ref[fin_col] = xn
        part = jnp.broadcast_to(jnp.sum(xn * xn, axis=-1, keepdims=True), ssq_ref.shape[1:])
        ssq_ref[half] = ssq_ref[half] + part

    def emit():
        half = lax.rem(emi_row, 2)
        ssq = ssq_ref[half]
        scale = lax.rsqrt(ssq[:, :1] * (1.0 / d_model) + EPS)
        o_ref[...] = xrow_ref[emi_col] * scale * g_ref[...]
        ssq_ref[half] = ssq * jnp.where(emi_col == n_tiles - 1, 0.0, 1.0)

    @pl.when(t == 0)
    def _():
        ssq_ref[...] = jnp.zeros(ssq_ref.shape, F32)
        matmul()

    @pl.when(jnp.logical_and(t >= 1, t <= n_tiles))
    def _():
        finish()
        matmul()

    @pl.when(jnp.logical_and(t > n_tiles, t < n_mm))
    def _():
        emit()
        finish()
        matmul()

    @pl.when(t == n_mm)
    def _():
        emit()
        finish()

    @pl.when(t > n_mm)
    def _():
        emit()


def _outproj_final(a, r, w, x2, g):
    m, d = x2.shape
    ka, kr = a.shape[1], r.shape[1]
    assert ka == kr and w.shape[0] == ka + kr
    tm = min(PROJ_TM, m)
    tn = min(PROJ_TN, d)
    n_tiles = d // tn
    n_mm = (m // tm) * n_tiles
    n_steps = n_mm + n_tiles + 1
    assert n_mm > n_tiles

    def cur(t):
        t = jnp.minimum(t, n_mm - 1)
        return t // n_tiles, lax.rem(t, n_tiles)

    def fin(t):
        t = jnp.clip(t - 1, 0, n_mm - 1)
        return t // n_tiles, lax.rem(t, n_tiles)

    def emi(t):
        t = jnp.maximum(t - 1 - n_tiles, 0)
        return t // n_tiles, lax.rem(t, n_tiles)

    return pl.pallas_call(
        functools.partial(_outproj_final_kernel, n_tiles, n_mm, d),
        grid=(n_steps,),
        in_specs=[pl.BlockSpec((tm, ka), lambda t: (cur(t)[0], 0)),
                  pl.BlockSpec((tm, kr), lambda t: (cur(t)[0], 0)),
                  pl.BlockSpec((ka, tn), lambda t: (0, cur(t)[1])),
                  pl.BlockSpec((kr, tn), lambda t: (1, cur(t)[1])),
                  pl.BlockSpec((tm, tn), lambda t: fin(t)),
                  pl.BlockSpec((1, tn), lambda t: (0, emi(t)[1]))],
        out_specs=pl.BlockSpec((tm, tn), lambda t: emi(t)),
        out_shape=jax.ShapeDtypeStruct((m, d), F32),
        scratch_shapes=[pltpu.VMEM((tm, tn), F32),
                        pltpu.VMEM((n_tiles, tm, tn), F32),
                        pltpu.VMEM((2, tm, LANES), F32)],
        compiler_params=_cparams("arbitrary"),
        name="outproj_final",
    )(a, r, w, w, x2, g)


def _rope_tables(seq):
    rows = seq // GRID_W
    row = jnp.repeat(jnp.arange(rows), GRID_W).astype(F32)
    col = jnp.tile(jnp.arange(GRID_W), rows).astype(F32)
    axis_dim = HEAD_DIM // 2
    inv = ROPE_THETA ** (-jnp.arange(0, axis_dim, 2, dtype=F32) / axis_dim)
    ang_r = row[:, None] * inv[None, :]
    ang_c = col[:, None] * inv[None, :]
    cr, sr, cc, sc = jnp.cos(ang_r), jnp.sin(ang_r), jnp.cos(ang_c), jnp.sin(ang_c)
    z = jnp.zeros_like(sr)
    cos = jnp.concatenate([cr, cr, cc, cc], axis=-1)
    sin_up = jnp.concatenate([-sr, z, -sc, z], axis=-1)
    sin_dn = jnp.concatenate([z, sr, z, sc], axis=-1)
    return cos, sin_up, sin_dn


def kernel(x, norm_w, w_in, q_norm, k_norm, ret_decay_fwd, ret_decay_bwd, ret_norm, w_out, final_norm):
    batch, seq, d = x.shape
    depth = w_in.shape[0]
    attn_width = d // 2
    kv_width = attn_width // ATTN_GROUP
    ret_width = d - attn_width
    ret_heads = ret_width // RET_V_DIM
    ret_qk_width = ret_heads * RET_QK_DIM
    m = batch * seq

    ref_widths = (("aq", attn_width), ("ak", kv_width), ("av", kv_width), ("ag", attn_width),
                  ("rq", ret_qk_width), ("rk", ret_qk_width), ("rv", ret_width), ("rg", ret_width))
    ref_pos, pos = {}, 0
    for name, wdt in ref_widths:
        ref_pos[name] = (pos, wdt)
        pos += wdt
    assert pos == w_in.shape[2]
    tn = min(PROJ_TN, kv_width)
    offs, kind_ranges, gain_starts, pos = {}, [], [], 0
    for name in SECTION_ORDER:
        wdt = ref_pos[name][1]
        assert wdt % tn == 0
        offs[name] = pos
        kind = SECTION_KIND[name]
        if kind_ranges and kind_ranges[-1][0] == kind:
            kind_ranges[-1] = (kind, kind_ranges[-1][1], (pos + wdt) // tn)
        else:
            kind_ranges.append((kind, pos // tn, (pos + wdt) // tn))
        if name in GAINED:
            gain_starts.append(pos // tn)
        pos += wdt

    tile_shifts = tuple((offs[name] // tn, (offs[name] + ref_pos[name][1]) // tn, (ref_pos[name][0] - offs[name]) // tn)
                        for name in SECTION_ORDER)
    tables = _rope_tables(seq)
    attn_scale = HEAD_DIM ** -0.5
    x2 = x.reshape(m, d)

    xg, ssq = _prologue(x2, norm_w[0].reshape(1, d))
    for l in range(depth):
        gains = jnp.stack([q_norm[l] * (attn_scale * LOG2E),
                           k_norm[l],
                           jnp.ones((HEAD_DIM,), F32),
                           jnp.full((HEAD_DIM,), RET_QK_DIM ** -0.5, F32)]).reshape(len(GAINED), 1, HEAD_DIM)
        proj = _inproj(xg, w_in, l, ssq, tables, gains, seq=seq, kind_ranges=tuple(kind_ranges),
                       gain_starts=tuple(gain_starts), tile_shifts=tile_shifts, tn=tn)
        a, w_out_b = _attention(proj, gains.reshape(len(GAINED), HEAD_DIM), w_out, l, batch=batch, seq=seq,
                                attn_width=attn_width, kv_width=kv_width, offs=offs)
        dfw = jnp.broadcast_to(ret_decay_fwd[l][:, None, None], (ret_heads, 1, LANES))
        dbw = jnp.broadcast_to(ret_decay_bwd[l][:, None, None], (ret_heads, 1, LANES))
        r = _retention(proj, dfw, dbw, ret_norm[l].reshape(ret_heads, 1, RET_V_DIM),
                       batch=batch, seq=seq, offs=offs, ret_heads=ret_heads)
        if l + 1 < depth:
            x2, xg, ssq = _outproj(a, r, w_out_b, x2, norm_w[l + 1].reshape(1, d))
        else:
            out = _outproj_final(a, r, w_out_b, x2, final_norm.reshape(1, d))
    return out.reshape(batch, seq, d)
```

```python
import functools
import math

import jax
import jax.numpy as jnp
from jax import lax
from jax.experimental import pallas as pl
from jax.experimental.pallas import tpu as pltpu

F32 = jnp.float32
BF16 = jnp.bfloat16

GRID_W = 64
HEAD_DIM = 128
ATTN_GROUP = 4
RET_V_DIM = 256
RET_QK_DIM = 128
ROPE_THETA = 10000.0
EPS = 1e-6
LOG2E = math.log2(math.e)

LANES = 128
BF16_ROWS = 16
VMEM_LIMIT = 56 * 1024 * 1024

PROJ_TM = 1024
PROJ_TN = 512
ATTN_TQ = 256
ATTN_TK = 1024
RET_CHUNK = 256
ROWWISE_TM = 256

NT_DIMS = (((1,), (1,)), ((), ()))


def _cparams(*sem):
    return pltpu.CompilerParams(dimension_semantics=sem, vmem_limit_bytes=VMEM_LIMIT)


def _prologue_kernel(x_ref, g_ref, xg_ref, ssq_ref):
    x = x_ref[...]
    xg_ref[...] = (x * g_ref[...]).astype(BF16)
    ssq_ref[...] = jnp.broadcast_to(jnp.sum(x * x, axis=-1, keepdims=True), ssq_ref.shape)


def _prologue(x2, g):
    m, d = x2.shape
    tm = min(ROWWISE_TM, m)
    return pl.pallas_call(
        _prologue_kernel,
        grid=(m // tm,),
        in_specs=[pl.BlockSpec((tm, d), lambda i: (i, 0)),
                  pl.BlockSpec((1, d), lambda i: (0, 0))],
        out_specs=[pl.BlockSpec((tm, d), lambda i: (i, 0)),
                   pl.BlockSpec((tm, LANES), lambda i: (i, 0))],
        out_shape=[jax.ShapeDtypeStruct((m, d), BF16),
                   jax.ShapeDtypeStruct((m, LANES), F32)],
        compiler_params=_cparams("parallel"),
        name="prologue",
    )(x2, g)


SECTION_ORDER = ("ag", "rg", "aq", "ak", "rq", "rk", "av", "rv")
SECTION_KIND = {"ag": "gate", "rg": "gate", "aq": "norm_rope", "ak": "norm_rope",
                "rq": "rope", "rk": "rope", "av": "plain", "rv": "plain"}
GAINED = ("aq", "ak", "rq", "rk")
ROW_CHUNK = 512


def _rope(y, cos, sin_up, sin_dn):
    q = HEAD_DIM // 4
    return y * cos + pltpu.roll(y, HEAD_DIM - q, 1) * sin_up + pltpu.roll(y, q, 1) * sin_dn


def _silu(y):
    h = 0.5 * y
    return h + h * jnp.tanh(h)


def _inproj_kernel(kind_ranges, n_tiles, n_steps, d_model, xg_ref, w_ref, ssq_ref, cos_ref, sup_ref, sdn_ref,
                   gain_ref, o_ref, acc_ref):
    t = pl.program_id(0)
    prev_col = lax.rem(t + (n_tiles - 1), n_tiles)
    tm, tn = acc_ref.shape

    def matmul():
        acc_ref[...] = jnp.dot(xg_ref[...], w_ref[...].astype(BF16), preferred_element_type=F32)

    def postprocess(kind):
        for c in range(tm // ROW_CHUNK):
            rows = slice(c * ROW_CHUNK, (c + 1) * ROW_CHUNK)
            r = lax.rsqrt(ssq_ref[rows, :] * (1.0 / d_model) + EPS)
            for g in range(tn // LANES):
                cols = slice(g * LANES, (g + 1) * LANES)
                y = acc_ref[rows, cols] * r
                if kind == "norm_rope":
                    ms = jnp.mean(y * y, axis=-1, keepdims=True)
                    y = _rope(y * lax.rsqrt(ms + EPS) * gain_ref[...],
                              cos_ref[rows, :], sup_ref[rows, :], sdn_ref[rows, :])
                elif kind == "rope":
                    y = _rope(y * gain_ref[...], cos_ref[rows, :], sup_ref[rows, :], sdn_ref[rows, :])
                elif kind == "gate":
                    y = _silu(y)
                o_ref[rows, cols] = y.astype(BF16)

    @pl.when(t == 0)
    def _():
        matmul()

    middle = jnp.logical_and(t > 0, t < n_steps - 1)
    for kind, lo, hi in kind_ranges:
        @pl.when(jnp.logical_and(middle, jnp.logical_and(prev_col >= lo, prev_col < hi)))
        def _(kind=kind):
            postprocess(kind)
            matmul()

    @pl.when(t == n_steps - 1)
    def _():
        postprocess(kind_ranges[-1][0])


def _inproj(xg, w, layer, ssq, tables, gains, *, seq, kind_ranges, gain_starts, tile_shifts, tn):
    m, d = xg.shape
    n = w.shape[2]
    n_tiles = n // tn
    tm = min(PROJ_TM, seq)
    sb = seq // tm
    n_steps = (m // tm) * n_tiles + 1
    assert kind_ranges[-1][2] == n_tiles and tm % ROW_CHUNK == 0

    def cur(t):
        t = jnp.minimum(t, n_steps - 2)
        return t // n_tiles, lax.rem(t, n_tiles)

    def prev(t):
        t = jnp.maximum(t - 1, 0)
        return t // n_tiles, lax.rem(t, n_tiles)

    def gain_index(t):
        col = prev(t)[1]
        return (sum((col >= s).astype(jnp.int32) for s in gain_starts[1:]), 0, 0)

    def weight_index(t):
        col = cur(t)[1]
        shift = sum(jnp.where(jnp.logical_and(col >= lo, col < hi), sh, 0) for lo, hi, sh in tile_shifts)
        return (layer, 0, col + shift)

    kern = functools.partial(_inproj_kernel, kind_ranges, n_tiles, n_steps, d)
    tab = pl.BlockSpec((tm, LANES), lambda t: (lax.rem(prev(t)[0], sb), 0))
    return pl.pallas_call(
        kern,
        grid=(n_steps,),
        in_specs=[pl.BlockSpec((tm, d), lambda t: (cur(t)[0], 0)),
                  pl.BlockSpec((None, d, tn), weight_index),
                  pl.BlockSpec((tm, LANES), lambda t: (prev(t)[0], 0)),
                  tab, tab, tab,
                  pl.BlockSpec((None, 1, LANES), gain_index)],
        out_specs=pl.BlockSpec((tm, tn), lambda t: prev(t)),
        out_shape=jax.ShapeDtypeStruct((m, n), BF16),
        scratch_shapes=[pltpu.VMEM((tm, tn), F32)],
        compiler_params=_cparams("arbitrary"),
        name="inproj",
    )(xg, w, ssq, *tables, gains)


def _transpose_via_mxu(a):
    return a.astype(F32).T.astype(BF16)


SCORE_LIMIT = 50.0
Q_UNROLL = 4


def _attn_kernel(tq, tk, q_ref, k_ref, v_ref, g_ref, gains_ref, w_ref, o_ref, wb_ref, vt_ref):
    wb_ref[...] = w_ref[...].astype(BF16)
    seq = q_ref.shape[0]
    nq = ATTN_GROUP * tq
    vt_ref[...] = _transpose_via_mxu(v_ref[...])
    gain_q = jnp.max(jnp.abs(gains_ref[0:1, :]))
    gain_k = jnp.max(jnp.abs(gains_ref[1:2, :]))
    bound = gain_q * gain_k * (HEAD_DIM * 1.02)

    def load_q(qi):
        rows = pl.ds(pl.multiple_of(qi * tq, tq), tq)
        qb = q_ref[rows, :]
        qs = jnp.concatenate([qb[:, g * LANES:(g + 1) * LANES] for g in range(ATTN_GROUP)], axis=0)
        return rows, qs

    def finish(rows, o):
        for g in range(ATTN_GROUP):
            cols = slice(g * LANES, (g + 1) * LANES)
            og = o[:, g * tq:(g + 1) * tq].T
            o_ref[rows, cols] = (g_ref[rows, cols].astype(F32) * og).astype(BF16)

    def unshifted():
        def qblock(qi, carry):
            rows, qs = load_q(qi)

            def kvstep(c, st):
                l, acc = st
                c0 = pl.multiple_of(c * tk, tk)
                s = lax.dot_general(k_ref[pl.ds(c0, tk), :], qs, NT_DIMS,
                                    preferred_element_type=F32)
                p = jnp.exp2(s)
                l = l + jnp.sum(p.reshape(tk // 8, 8, nq), axis=0)
                return l, acc + jnp.dot(vt_ref[:, pl.ds(c0, tk)], p.astype(BF16), preferred_element_type=F32)

            init = (jnp.zeros((8, nq), F32), jnp.zeros((HEAD_DIM, nq), F32))
            l, acc = lax.fori_loop(0, seq // tk, kvstep, init, unroll=True)
            finish(rows, acc * (1.0 / jnp.sum(l, axis=0, keepdims=True)))
            return carry

        lax.fori_loop(0, seq // tq, qblock, 0, unroll=Q_UNROLL)

    def running_max():
        def qblock(qi, carry):
            rows, qs = load_q(qi)

            def kvstep(c, st):
                m, l, acc = st
                c0 = pl.multiple_of(c * tk, tk)
                s = lax.dot_general(k_ref[pl.ds(c0, tk), :], qs, NT_DIMS,
                                    preferred_element_type=F32)
                m_new = jnp.maximum(m, jnp.max(s, axis=0, keepdims=True))
                alpha = jnp.exp2(m - m_new)
                p = jnp.exp2(s - m_new)
                l = alpha * l + jnp.sum(p, axis=0, keepdims=True)
                pv = jnp.dot(vt_ref[:, pl.ds(c0, tk)], p.astype(BF16), preferred_element_type=F32)
                return m_new, l, alpha * acc + pv

            init = (jnp.full((1, nq), -1e30, F32), jnp.zeros((1, nq), F32), jnp.zeros((HEAD_DIM, nq), F32))
            _, l, acc = lax.fori_loop(0, seq // tk, kvstep, init)
            finish(rows, acc * (1.0 / l))
            return carry

        lax.fori_loop(0, seq // tq, qblock, 0)

    lax.cond(bound <= SCORE_LIMIT, unshifted, running_max)


def _attention(proj, gains, w_out, layer, *, batch, seq, attn_width, kv_width, offs):
    kvh = kv_width // HEAD_DIM
    wrows, wcols = w_out.shape[1:]
    slab = wrows // (batch * kvh)
    assert slab * batch * kvh == wrows and slab % BF16_ROWS == 0
    gw = ATTN_GROUP * HEAD_DIM
    tq = min(ATTN_TQ, seq)
    tk = min(ATTN_TK, seq)
    assert offs["aq"] % gw == 0 and offs["ag"] % gw == 0
    q_off = offs["aq"] // gw
    k_off = offs["ak"] // HEAD_DIM
    v_off = offs["av"] // HEAD_DIM
    g_off = offs["ag"] // gw
    return pl.pallas_call(
        functools.partial(_attn_kernel, tq, tk),
        grid=(batch, kvh),
        in_specs=[pl.BlockSpec((seq, gw), lambda b, h: (b, q_off + h)),
                  pl.BlockSpec((seq, HEAD_DIM), lambda b, h: (b, k_off + h)),
                  pl.BlockSpec((seq, HEAD_DIM), lambda b, h: (b, v_off + h)),
                  pl.BlockSpec((seq, gw), lambda b, h: (b, g_off + h)),
                  pl.BlockSpec(gains.shape, lambda b, h: (0, 0)),
                  pl.BlockSpec((None, slab, wcols), lambda b, h: (layer, b * kvh + h, 0))],
        out_specs=[pl.BlockSpec((seq, gw), lambda b, h: (b, h)),
                   pl.BlockSpec((slab, wcols), lambda b, h: (b * kvh + h, 0))],
        out_shape=[jax.ShapeDtypeStruct((batch * seq, attn_width), BF16),
                   jax.ShapeDtypeStruct((wrows, wcols), BF16)],
        scratch_shapes=[pltpu.VMEM((HEAD_DIM, seq), BF16)],
        compiler_params=_cparams("parallel", "parallel"),
        name="attention",
    )(proj, proj, proj, proj, gains, w_out)


RET_UNROLL = 16
RET_HEADS_PER_STEP = 2


def _log_sigmoid(x):
    return jnp.minimum(x, 0.0) - jnp.log1p(jnp.exp(-jnp.abs(x)))


def _retention_kernel(ck, q_ref, k_ref, v_ref, g_ref, df_ref, db_ref, gn_ref, o_ref, kt_ref, sb_ref):
    seq = q_ref.shape[0]
    nchunk = seq // ck
    heads = range(RET_HEADS_PER_STEP)
    qk = lambda e: slice(e * RET_QK_DIM, (e + 1) * RET_QK_DIM)
    vv = lambda e: slice(e * RET_V_DIM, (e + 1) * RET_V_DIM)

    diff = (lax.broadcasted_iota(jnp.int32, (ck, ck), 0)
            - lax.broadcasted_iota(jnp.int32, (ck, ck), 1)).astype(F32)
    rowi = lax.broadcasted_iota(jnp.int32, (ck, RET_QK_DIM), 0).astype(F32)
    coli = lax.broadcasted_iota(jnp.int32, (RET_QK_DIM, ck), 1).astype(F32)
    decay, xi_f, xi_b, zt_f, zt_b, gc_f, gc_b = [], [], [], [], [], [], []
    for e in heads:
        lgf = _log_sigmoid(df_ref[e])[:, :1]
        lgb = _log_sigmoid(db_ref[e])[:, :1]
        kt_ref[e] = _transpose_via_mxu(k_ref[:, qk(e)])
        decay.append(jnp.exp(jnp.where(diff >= 0, lgf * diff, -lgb * diff)))
        xi_f.append(jnp.exp(lgf * (rowi + 1.0)))
        xi_b.append(jnp.exp(lgb * (ck - rowi)))
        zt_f.append(jnp.exp(lgf * (ck - 1.0 - coli)))
        zt_b.append(jnp.exp(lgb * coli))
        gc_f.append(jnp.exp(lgf * ck))
        gc_b.append(jnp.exp(lgb * ck))

    def kz(e, c0, zt):
        return (kt_ref[e, :, pl.ds(c0, ck)].astype(F32) * zt).astype(BF16)

    def bstep(t, states):
        jc = nchunk - 1 - t
        c0 = pl.multiple_of(jc * ck, ck)
        new = []
        for e in heads:
            sb_ref[e, jc] = states[e].astype(BF16)
            upd = jnp.dot(kz(e, c0, zt_b[e]), v_ref[pl.ds(c0, ck), vv(e)], preferred_element_type=F32)
            new.append(gc_b[e] * states[e] + upd)
        return tuple(new)

    zeros = tuple(jnp.zeros((RET_QK_DIM, RET_V_DIM), F32) for _ in heads)
    lax.fori_loop(0, nchunk, bstep, zeros, unroll=RET_UNROLL)

    def fstep(jc, states):
        c0 = pl.multiple_of(jc * ck, ck)
        rows = pl.ds(c0, ck)
        new = []
        for e in heads:
            q = q_ref[rows, qk(e)]
            v = v_ref[rows, vv(e)]
            s = jnp.dot(q, kt_ref[e, :, pl.ds(c0, ck)], preferred_element_type=F32)
            out = jnp.dot((s * decay[e]).astype(BF16), v, preferred_element_type=F32)
            qf = q.astype(F32)
            qx = jnp.concatenate([(qf * xi_f[e]).astype(BF16), (qf * xi_b[e]).astype(BF16)], axis=1)
            both = jnp.concatenate([states[e].astype(BF16), sb_ref[e, jc]], axis=0)
            out = out + jnp.dot(qx, both, preferred_element_type=F32)
            ms = jnp.mean(out * out, axis=-1, keepdims=True)
            y = out * lax.rsqrt(ms + EPS) * gn_ref[e]
            o_ref[rows, vv(e)] = (g_ref[rows, vv(e)].astype(F32) * y).astype(BF16)
            upd = jnp.dot(kz(e, c0, zt_f[e]), v, preferred_element_type=F32)
            new.append(gc_f[e] * states[e] + upd)
        return tuple(new)

    lax.fori_loop(0, nchunk, fstep, zeros, unroll=RET_UNROLL)


def _retention(proj, dfw, dbw, gnorm, *, batch, seq, offs, ret_heads):
    hps = RET_HEADS_PER_STEP
    wq, wv = hps * RET_QK_DIM, hps * RET_V_DIM
    off_q, off_k, off_v, off_g = offs["rq"], offs["rk"], offs["rv"], offs["rg"]
    assert ret_heads % hps == 0 and off_q % wq == 0 and off_k % wq == 0 and off_v % wv == 0 and off_g % wv == 0
    ck = min(RET_CHUNK, seq)
    vec = lambda w: pl.BlockSpec((hps, 1, w), lambda b, h: (h, 0, 0))
    return pl.pallas_call(
        functools.partial(_retention_kernel, ck),
        grid=(batch, ret_heads // hps),
        in_specs=[pl.BlockSpec((seq, wq), lambda b, h: (b, off_q // wq + h)),
                  pl.BlockSpec((seq, wq), lambda b, h: (b, off_k // wq + h)),
                  pl.BlockSpec((seq, wv), lambda b, h: (b, off_v // wv + h)),
                  pl.BlockSpec((seq, wv), lambda b, h: (b, off_g // wv + h)),
                  vec(LANES), vec(LANES), vec(RET_V_DIM)],
        out_specs=pl.BlockSpec((seq, wv), lambda b, h: (b, h)),
        out_shape=jax.ShapeDtypeStruct((batch * seq, ret_heads * RET_V_DIM), BF16),
        scratch_shapes=[pltpu.VMEM((hps, RET_QK_DIM, seq), BF16),
                        pltpu.VMEM((hps, seq // ck, RET_QK_DIM, RET_V_DIM), BF16)],
        compiler_params=_cparams("parallel", "parallel"),
        name="retention",
    )(proj, proj, proj, proj, dfw, dbw, gnorm)


def _outproj_kernel(n_tiles, n_steps, a_ref, r_ref, wa_ref, wr_ref, x_ref, g_ref, xo_ref, xg_ref, ssq_ref, acc_ref):
    t = pl.program_id(0)
    prev_col = lax.rem(t + (n_tiles - 1), n_tiles)

    def matmul():
        acc = jnp.dot(a_ref[...], wa_ref[...], preferred_element_type=F32)
        acc_ref[...] = acc + jnp.dot(r_ref[...], wr_ref[...], preferred_element_type=F32)

    def finish():
        xn = x_ref[...] + acc_ref[...]
        xo_ref[...] = xn
        xg_ref[...] = (xn * g_ref[...]).astype(BF16)
        return jnp.broadcast_to(jnp.sum(xn * xn, axis=-1, keepdims=True), ssq_ref.shape)

    def accumulate(part):
        @pl.when(prev_col == 0)
        def _():
            ssq_ref[...] = part

        @pl.when(prev_col > 0)
        def _():
            ssq_ref[...] = ssq_ref[...] + part

    @pl.when(t == 0)
    def _():
        matmul()

    @pl.when(jnp.logical_and(t > 0, t < n_steps - 1))
    def _():
        part = finish()
        matmul()
        accumulate(part)

    @pl.when(t == n_steps - 1)
    def _():
        accumulate(finish())


def _outproj(a, r, w, x2, gnext):
    m, d = x2.shape
    ka, kr = a.shape[1], r.shape[1]
    assert ka == kr and w.shape[0] == ka + kr
    tm = min(PROJ_TM, m)
    tn = min(PROJ_TN, d)
    n_tiles = d // tn
    n_steps = (m // tm) * n_tiles + 1

    def cur(t):
        t = jnp.minimum(t, n_steps - 2)
        return t // n_tiles, lax.rem(t, n_tiles)

    def prev(t):
        t = jnp.maximum(t - 1, 0)
        return t // n_tiles, lax.rem(t, n_tiles)

    return pl.pallas_call(
        functools.partial(_outproj_kernel, n_tiles, n_steps),
        grid=(n_steps,),
        in_specs=[pl.BlockSpec((tm, ka), lambda t: (cur(t)[0], 0)),
                  pl.BlockSpec((tm, kr), lambda t: (cur(t)[0], 0)),
                  pl.BlockSpec((ka, tn), lambda t: (0, cur(t)[1])),
                  pl.BlockSpec((kr, tn), lambda t: (1, cur(t)[1])),
                  pl.BlockSpec((tm, tn), lambda t: prev(t)),
                  pl.BlockSpec((1, tn), lambda t: (0, prev(t)[1]))],
        out_specs=[pl.BlockSpec((tm, tn), lambda t: prev(t)),
                   pl.BlockSpec((tm, tn), lambda t: prev(t)),
                   pl.BlockSpec((tm, LANES), lambda t: (prev(t)[0], 0))],
        out_shape=[jax.ShapeDtypeStruct((m, d), F32),
                   jax.ShapeDtypeStruct((m, d), BF16),
                   jax.ShapeDtypeStruct((m, LANES), F32)],
        scratch_shapes=[pltpu.VMEM((tm, tn), F32)],
        compiler_params=_cparams("arbitrary"),
        name="outproj",
    )(a, r, w, w, x2, gnext)


def _outproj_final_kernel(n_tiles, n_mm, d_model, a_ref, r_ref, wa_ref, wr_ref, x_ref, g_ref, o_ref,
                          acc_ref, xrow_ref, ssq_ref):
    t = pl.program_id(0)
    fin = jnp.maximum(t - 1, 0)
    fin_row, fin_col = fin // n_tiles, lax.rem(fin, n_tiles)
    emi = jnp.maximum(t - 1 - n_tiles, 0)
    emi_row, emi_col = emi // n_tiles, lax.rem(emi, n_tiles)

    def matmul():
        acc = jnp.dot(a_ref[...], wa_ref[...], preferred_element_type=F32)
        acc_ref[...] = acc + jnp.dot(r_ref[...], wr_ref[...], preferred_element_type=F32)

    def finish():
        half = lax.rem(fin_row, 2)
        xn = x_ref[...] + acc_ref[...]
        xrow_ref[fin_col] = xn
        part = jnp.broadcast_to(jnp.sum(xn * xn, axis=-1, keepdims=True), ssq_ref.shape[1:])
        ssq_ref[half] = ssq_ref[half] + part

    def emit():
        half = lax.rem(emi_row, 2)
        ssq = ssq_ref[half]
        scale = lax.rsqrt(ssq[:, :1] * (1.0 / d_model) + EPS)
        o_ref[...] = xrow_ref[emi_col] * scale * g_ref[...]
        ssq_ref[half] = ssq * jnp.where(emi_col == n_tiles - 1, 0.0, 1.0)

    @pl.when(t == 0)
    def _():
        ssq_ref[...] = jnp.zeros(ssq_ref.shape, F32)
        matmul()

    @pl.when(jnp.logical_and(t >= 1, t <= n_tiles))
    def _():
        finish()
        matmul()

    @pl.when(jnp.logical_and(t > n_tiles, t < n_mm))
    def _():
        emit()
        finish()
        matmul()

    @pl.when(t == n_mm)
    def _():
        emit()
        finish()

    @pl.when(t > n_mm)
    def _():
        emit()


def _outproj_final(a, r, w, x2, g):
    m, d = x2.shape
    ka, kr = a.shape[1], r.shape[1]
    assert ka == kr and w.shape[0] == ka + kr
    tm = min(PROJ_TM, m)
    tn = min(PROJ_TN, d)
    n_tiles = d // tn
    n_mm = (m // tm) * n_tiles
    n_steps = n_mm + n_tiles + 1
    assert n_mm > n_tiles

    def cur(t):
        t = jnp.minimum(t, n_mm - 1)
        return t // n_tiles, lax.rem(t, n_tiles)

    def fin(t):
        t = jnp.clip(t - 1, 0, n_mm - 1)
        return t // n_tiles, lax.rem(t, n_tiles)

    def emi(t):
        t = jnp.maximum(t - 1 - n_tiles, 0)
        return t // n_tiles, lax.rem(t, n_tiles)

    return pl.pallas_call(
        functools.partial(_outproj_final_kernel, n_tiles, n_mm, d),
        grid=(n_steps,),
        in_specs=[pl.BlockSpec((tm, ka), lambda t: (cur(t)[0], 0)),
                  pl.BlockSpec((tm, kr), lambda t: (cur(t)[0], 0)),
                  pl.BlockSpec((ka, tn), lambda t: (0, cur(t)[1])),
                  pl.BlockSpec((kr, tn), lambda t: (1, cur(t)[1])),
                  pl.BlockSpec((tm, tn), lambda t: fin(t)),
                  pl.BlockSpec((1, tn), lambda t: (0, emi(t)[1]))],
        out_specs=pl.BlockSpec((tm, tn), lambda t: emi(t)),
        out_shape=jax.ShapeDtypeStruct((m, d), F32),
        scratch_shapes=[pltpu.VMEM((tm, tn), F32),
                        pltpu.VMEM((n_tiles, tm, tn), F32),
                        pltpu.VMEM((2, tm, LANES), F32)],
        compiler_params=_cparams("arbitrary"),
        name="outproj_final",
    )(a, r, w, w, x2, g)


def _rope_tables(seq):
    rows = seq // GRID_W
    row = jnp.repeat(jnp.arange(rows), GRID_W).astype(F32)
    col = jnp.tile(jnp.arange(GRID_W), rows).astype(F32)
    axis_dim = HEAD_DIM // 2
    inv = ROPE_THETA ** (-jnp.arange(0, axis_dim, 2, dtype=F32) / axis_dim)
    ang_r = row[:, None] * inv[None, :]
    ang_c = col[:, None] * inv[None, :]
    cr, sr, cc, sc = jnp.cos(ang_r), jnp.sin(ang_r), jnp.cos(ang_c), jnp.sin(ang_c)
    z = jnp.zeros_like(sr)
    cos = jnp.concatenate([cr, cr, cc, cc], axis=-1)
    sin_up = jnp.concatenate([-sr, z, -sc, z], axis=-1)
    sin_dn = jnp.concatenate([z, sr, z, sc], axis=-1)
    return cos, sin_up, sin_dn


def kernel(x, norm_w, w_in, q_norm, k_norm, ret_decay_fwd, ret_decay_bwd, ret_norm, w_out, final_norm):
    batch, seq, d = x.shape
    depth = w_in.shape[0]
    attn_width = d // 2
    kv_width = attn_width // ATTN_GROUP
    ret_width = d - attn_width
    ret_heads = ret_width // RET_V_DIM
    ret_qk_width = ret_heads * RET_QK_DIM
    m = batch * seq

    ref_widths = (("aq", attn_width), ("ak", kv_width), ("av", kv_width), ("ag", attn_width),
                  ("rq", ret_qk_width), ("rk", ret_qk_width), ("rv", ret_width), ("rg", ret_width))
    ref_pos, pos = {}, 0
    for name, wdt in ref_widths:
        ref_pos[name] = (pos, wdt)
        pos += wdt
    assert pos == w_in.shape[2]
    tn = min(PROJ_TN, kv_width)
    offs, kind_ranges, gain_starts, pos = {}, [], [], 0
    for name in SECTION_ORDER:
        wdt = ref_pos[name][1]
        assert wdt % tn == 0
        offs[name] = pos
        kind = SECTION_KIND[name]
        if kind_ranges and kind_ranges[-1][0] == kind:
            kind_ranges[-1] = (kind, kind_ranges[-1][1], (pos + wdt) // tn)
        else:
            kind_ranges.append((kind, pos // tn, (pos + wdt) // tn))
        if name in GAINED:
            gain_starts.append(pos // tn)
        pos += wdt

    tile_shifts = tuple((offs[name] // tn, (offs[name] + ref_pos[name][1]) // tn, (ref_pos[name][0] - offs[name]) // tn)
                        for name in SECTION_ORDER)
    tables = _rope_tables(seq)
    attn_scale = HEAD_DIM ** -0.5
    x2 = x.reshape(m, d)

    xg, ssq = _prologue(x2, norm_w[0].reshape(1, d))
    for l in range(depth):
        gains = jnp.stack([q_norm[l] * (attn_scale * LOG2E),
                           k_norm[l],
                           jnp.ones((HEAD_DIM,), F32),
                           jnp.full((HEAD_DIM,), RET_QK_DIM ** -0.5, F32)]).reshape(len(GAINED), 1, HEAD_DIM)
        proj = _inproj(xg, w_in, l, ssq, tables, gains, seq=seq, kind_ranges=tuple(kind_ranges),
                       gain_starts=tuple(gain_starts), tile_shifts=tile_shifts, tn=tn)
        a, w_out_b = _attention(proj, gains.reshape(len(GAINED), HEAD_DIM), w_out, l, batch=batch, seq=seq,
                                attn_width=attn_width, kv_width=kv_width, offs=offs)
        dfw = jnp.broadcast_to(ret_decay_fwd[l][:, None, None], (ret_heads, 1, LANES))
        dbw = jnp.broadcast_to(ret_decay_bwd[l][:, None, None], (ret_heads, 1, LANES))
        r = _retention(proj, dfw, dbw, ret_norm[l].reshape(ret_heads, 1, RET_V_DIM),
                       batch=batch, seq=seq, offs=offs, ret_heads=ret_heads)
        if l + 1 < depth:
            x2, xg, ssq = _outproj(a, r, w_out_b, x2, norm_w[l + 1].reshape(1, d))
        else:
            out = _outproj_final(a, r, w_out_b, x2, final_norm.reshape(1, d))
    return out.reshape(batch, seq, d)
```
